```python
import jax, jax.numpy as jnp
from jax import lax
import numpy as np

D_MODEL = 2048
BATCH = 2
SEQ = 4096
DEPTH = 4
DEC_BATCH = 8
DEC_SEQ = 8
PAST_LEN = 16384
PAGE_SIZE = 128

HEAD_DIM = 128
SB_HEADS = 6
DSA_HEADS = 6
MEM_HEADS = 4
SB_WIDTH = SB_HEADS * HEAD_DIM
DSA_WIDTH = DSA_HEADS * HEAD_DIM
MEM_WIDTH = MEM_HEADS * HEAD_DIM
IDX_HEADS = 16
IDX_DIM = 64
MEM_LEN = 256
TOPK_MAX = 256
Q_BLOCK = 128
ROPE_THETA = 10000.0
NORM_EPS = 1e-6
ATTN_SCALE = HEAD_DIM ** -0.5
IDX_SCALE = (IDX_HEADS * IDX_DIM) ** -0.5
IN_SPLITS = (SB_WIDTH, SB_WIDTH, SB_WIDTH, SB_WIDTH,
             DSA_WIDTH, DSA_WIDTH, DSA_WIDTH, DSA_WIDTH,
             IDX_HEADS * IDX_DIM, IDX_DIM, IDX_HEADS,
             MEM_WIDTH, MEM_WIDTH,
             D_MODEL, D_MODEL, D_MODEL)
IN_WIDTH = 4 * SB_WIDTH + 4 * DSA_WIDTH + IDX_HEADS * IDX_DIM + IDX_DIM + IDX_HEADS + 2 * MEM_WIDTH + 3 * D_MODEL

kernel_name = 'stick_breaking_dsa_gated_hybrid'


def rmsnorm(x, g):
    xf = x.astype(jnp.float32)
    y = xf * lax.rsqrt(jnp.mean(xf * xf, axis=-1, keepdims=True) + NORM_EPS)
    return (y * g.astype(jnp.float32)).astype(x.dtype)


def rope(x, pos):
    half = x.shape[-1] // 2
    inv_freq = jnp.power(jnp.float32(ROPE_THETA), -jnp.arange(half, dtype=jnp.float32) / half)
    ang = pos.astype(jnp.float32)[:, None] * inv_freq[None, :]
    ang = ang.reshape((ang.shape[0],) + (1,) * (x.ndim - 3) + (half,))
    cos, sin = jnp.cos(ang), jnp.sin(ang)
    xf = x.astype(jnp.float32)
    x1, x2 = xf[..., :half], xf[..., half:]
    return jnp.concatenate([x1 * cos - x2 * sin, x2 * cos + x1 * sin], axis=-1).astype(x.dtype)


def in_project(x, g, w_in):
    b, t = x.shape[:2]
    h = rmsnorm(x, g)
    z = jnp.einsum('btd,dn->btn', h, w_in)
    offs = np.cumsum(IN_SPLITS)[:-1].tolist()
    (q_sb, k_sb, v_sb, z_sb, q_d, k_d, v_d, z_d, qi, ki, wi, q_m, z_m,
     g_sb, g_d, g_m) = jnp.split(z, offs, axis=-1)
    heads = lambda a, n: a.reshape(b, t, n, -1)
    return dict(q_sb=heads(q_sb, SB_HEADS), k_sb=heads(k_sb, SB_HEADS), v_sb=heads(v_sb, SB_HEADS), z_sb=z_sb,
                q_d=heads(q_d, DSA_HEADS), k_d=heads(k_d, DSA_HEADS), v_d=heads(v_d, DSA_HEADS), z_d=z_d,
                qi=heads(qi, IDX_HEADS), ki=ki, wi=wi,
                q_m=heads(q_m, MEM_HEADS), z_m=z_m,
                g_sb=g_sb, g_d=g_d, g_m=g_m)


def to_blocks(a):
    b, n = a.shape[:2]
    return jnp.moveaxis(a.reshape((b, n // Q_BLOCK, Q_BLOCK) + a.shape[2:]), 1, 0)


def from_blocks(a):
    a = jnp.moveaxis(a, 0, 1)
    return a.reshape((a.shape[0], a.shape[1] * a.shape[2]) + a.shape[3:])


def sb_attend(q, k, v, q_pos):
    k_pos = jnp.arange(k.shape[1])
    z = jnp.einsum('bqhd,bshd->bhqs', q, k).astype(jnp.float32) * ATTN_SCALE
    before = (k_pos[None, :] < q_pos[:, None])[None, None]
    log_keep = jnp.where(before, jax.nn.log_sigmoid(-z), 0.0)
    between = lax.cumsum(log_keep, axis=3, reverse=True) - log_keep
    a = jnp.where(before, jnp.exp(jax.nn.log_sigmoid(z) + between), 0.0)
    return jnp.einsum('bhqs,bshd->bqhd', a, v.astype(jnp.float32)).astype(q.dtype)


def dsa_attend(q, qi, wi, ki, q_pos, gather, topk):
    k_pos = jnp.arange(ki.shape[1])
    rel = jax.nn.relu(jnp.einsum('bqhe,bse->bhqs', qi, ki).astype(jnp.float32))
    score = jnp.einsum('bhqs,bqh->bqs', rel, wi.astype(jnp.float32)) * IDX_SCALE
    visible = (k_pos[None, :] <= q_pos[:, None])[None]
    score = jnp.where(visible, score, -jnp.inf)
    _, idx = lax.top_k(score, topk)
    k_sel, v_sel = gather(idx)
    valid = (idx <= q_pos[None, :, None])[:, None]
    logits = jnp.einsum('bqhd,bqkhd->bhqk', q, k_sel).astype(jnp.float32) * ATTN_SCALE
    p = jax.nn.softmax(jnp.where(valid, logits, -jnp.inf), axis=-1)
    return jnp.einsum('bhqk,bqkhd->bqhd', p, v_sel.astype(jnp.float32)).astype(q.dtype)


def mem_attend(q, mk, mv):
    s = jnp.einsum('bqhd,bmhd->bhqm', q, mk).astype(jnp.float32) * ATTN_SCALE
    p = jax.nn.softmax(s, axis=-1)
    return jnp.einsum('bhqm,bmhd->bqhd', p, mv.astype(jnp.float32)).astype(q.dtype)


def dense_gather(k, v):
    take = jax.vmap(lambda rows, i: rows[i])
    return lambda idx: (take(k, idx), take(v, idx))


def paged_gather(pool_k, pool_v, new_k, new_v, page_table):
    n_past = page_table.shape[1] * PAGE_SIZE
    take = jax.vmap(lambda rows, i: rows[i])

    def gather(idx):
        past = jnp.minimum(idx, n_past - 1)
        phys = take(page_table, past // PAGE_SIZE)
        off = past % PAGE_SIZE
        new_i = jnp.clip(idx - n_past, 0, new_k.shape[1] - 1)
        is_past = (idx < n_past)[..., None, None]
        k = jnp.where(is_past, pool_k[phys, off], take(new_k, new_i))
        v = jnp.where(is_past, pool_v[phys, off], take(new_v, new_i))
        return k, v
    return gather


def paged_rows(pool, page_table):
    rows = pool[page_table]
    return rows.reshape((page_table.shape[0], page_table.shape[1] * PAGE_SIZE) + pool.shape[2:])


def merge(x, p, o_sb, o_d, o_m, w_up_sb, w_up_dsa, w_up_mem, w_out):
    b, t = x.shape[:2]
    a = jnp.einsum('btc,cd->btd', o_sb.reshape(b, t, -1) * jax.nn.silu(p['z_sb']), w_up_sb)
    c = jnp.einsum('btc,cd->btd', o_d.reshape(b, t, -1) * jax.nn.silu(p['z_d']), w_up_dsa)
    m = jnp.einsum('btc,cd->btd', o_m.reshape(b, t, -1) * jax.nn.silu(p['z_m']), w_up_mem)
    mixed = jax.nn.sigmoid(p['g_sb']) * a + jax.nn.sigmoid(p['g_d']) * c + jax.nn.sigmoid(p['g_m']) * m
    return x + jnp.einsum('btd,de->bte', mixed, w_out)


def setup_inputs(seed: int = 0) -> dict:
    key = jax.random.key(seed)
    ks = jax.random.split(key, 24)
    f32 = jnp.float32
    n_pages = PAST_LEN // PAGE_SIZE
    n_used = DEC_BATCH * n_pages
    n_pool = n_used + max(1, n_used // 4)
    nrm = lambda k, shape, s=1.0: s * jax.random.normal(k, shape, f32)
    page_table = jax.random.permutation(ks[9], n_pool)[:n_used].reshape(DEC_BATCH, n_pages).astype(jnp.int32)
    return {
        'x_prompt': nrm(ks[0], (BATCH, SEQ, D_MODEL)),
        'x_sample': nrm(ks[1], (DEC_BATCH, DEC_SEQ, D_MODEL)),
        'cache_sb_k': nrm(ks[2], (DEPTH, n_pool, PAGE_SIZE, SB_HEADS, HEAD_DIM)),
        'cache_sb_v': nrm(ks[3], (DEPTH, n_pool, PAGE_SIZE, SB_HEADS, HEAD_DIM)),
        'cache_dsa_k': nrm(ks[4], (DEPTH, n_pool, PAGE_SIZE, DSA_HEADS, HEAD_DIM)),
        'cache_dsa_v': nrm(ks[5], (DEPTH, n_pool, PAGE_SIZE, DSA_HEADS, HEAD_DIM)),
        'cache_idx_k': nrm(ks[6], (DEPTH, n_pool, PAGE_SIZE, IDX_DIM)),
        'cache_mem_k': nrm(ks[7], (DEPTH, DEC_BATCH, MEM_LEN, MEM_HEADS, HEAD_DIM)),
        'cache_mem_v': nrm(ks[8], (DEPTH, DEC_BATCH, MEM_LEN, MEM_HEADS, HEAD_DIM)),
        'page_table': page_table,
        'mem_prompt': nrm(ks[10], (BATCH, MEM_LEN, D_MODEL)),
        'norm_g': 1.0 + nrm(ks[11], (DEPTH, D_MODEL), 0.02),
        'w_in': nrm(ks[12], (DEPTH, D_MODEL, IN_WIDTH), D_MODEL ** -0.5),
        'w_up_sb': nrm(ks[13], (DEPTH, SB_WIDTH, D_MODEL), SB_WIDTH ** -0.5),
        'w_up_dsa': nrm(ks[14], (DEPTH, DSA_WIDTH, D_MODEL), DSA_WIDTH ** -0.5),
        'w_up_mem': nrm(ks[15], (DEPTH, MEM_WIDTH, D_MODEL), MEM_WIDTH ** -0.5),
        'w_out': nrm(ks[16], (DEPTH, D_MODEL, D_MODEL), D_MODEL ** -0.5),
        'w_mem_k': nrm(ks[17], (DEPTH, D_MODEL, MEM_WIDTH), D_MODEL ** -0.5),
        'w_mem_v': nrm(ks[18], (DEPTH, D_MODEL, MEM_WIDTH), D_MODEL ** -0.5),
        'final_norm_g': 1.0 + nrm(ks[19], (D_MODEL,), 0.02),
    }


def reference(x_prompt, x_sample, cache_sb_k, cache_sb_v, cache_dsa_k, cache_dsa_v, cache_idx_k,
              cache_mem_k, cache_mem_v, page_table, mem_prompt, norm_g, w_in, w_up_sb, w_up_dsa,
              w_up_mem, w_out, w_mem_k, w_mem_v, final_norm_g):
    b, t = x_prompt.shape[:2]
    db, ds = x_sample.shape[:2]
    n_past = page_table.shape[1] * PAGE_SIZE
    pos_p = jnp.arange(t)
    pos_p_blocks = pos_p.reshape(t // Q_BLOCK, Q_BLOCK)
    pos_s = n_past + jnp.arange(ds)
    topk_p = min(TOPK_MAX, t // 4)
    topk_s = min(TOPK_MAX, (n_past + ds) // 4)

    p_sb_k, p_sb_v, p_dsa_k, p_dsa_v, p_idx_k, p_mem_k, p_mem_v = [], [], [], [], [], [], []
    x = x_prompt
    for l in range(DEPTH):
        p = in_project(x, norm_g[l], w_in[l])
        k_sb, v_sb = p['k_sb'], p['v_sb']
        q_d, k_d = rope(p['q_d'], pos_p), rope(p['k_d'], pos_p)
        qi, ki = rope(p['qi'], pos_p), rope(p['ki'], pos_p)
        v_d = p['v_d']
        o_sb = from_blocks(lax.map(lambda a: sb_attend(a[0], k_sb, v_sb, a[1]),
                                   (to_blocks(p['q_sb']), pos_p_blocks)))
        gather = dense_gather(k_d, v_d)
        o_d = from_blocks(lax.map(lambda a: dsa_attend(a[0], a[1], a[2], ki, a[3], gather, topk_p),
                                  (to_blocks(q_d), to_blocks(qi), to_blocks(p['wi']), pos_p_blocks)))
        mk = jnp.einsum('bmd,dc->bmc', mem_prompt, w_mem_k[l]).reshape(b, MEM_LEN, MEM_HEADS, HEAD_DIM)
        mv = jnp.einsum('bmd,dc->bmc', mem_prompt, w_mem_v[l]).reshape(b, MEM_LEN, MEM_HEADS, HEAD_DIM)
        o_m = mem_attend(p['q_m'], mk, mv)
        x = merge(x, p, o_sb, o_d, o_m, w_up_sb[l], w_up_dsa[l], w_up_mem[l], w_out[l])
        p_sb_k.append(k_sb); p_sb_v.append(v_sb)
        p_dsa_k.append(k_d); p_dsa_v.append(v_d); p_idx_k.append(ki)
        p_mem_k.append(mk); p_mem_v.append(mv)
    y_prompt = rmsnorm(x, final_norm_g)

    s_sb_k, s_sb_v, s_dsa_k, s_dsa_v, s_idx_k = [], [], [], [], []
    x = x_sample
    for l in range(DEPTH):
        p = in_project(x, norm_g[l], w_in[l])
        k_sb, v_sb = p['k_sb'], p['v_sb']
        q_d, k_d = rope(p['q_d'], pos_s), rope(p['k_d'], pos_s)
        qi, ki = rope(p['qi'], pos_s), rope(p['ki'], pos_s)
        v_d = p['v_d']
        k_sb_all = jnp.concatenate([paged_rows(cache_sb_k[l], page_table), k_sb], axis=1)
        v_sb_all = jnp.concatenate([paged_rows(cache_sb_v[l], page_table), v_sb], axis=1)
        o_sb = sb_attend(p['q_sb'], k_sb_all, v_sb_all, pos_s)
        ki_all = jnp.concatenate([paged_rows(cache_idx_k[l], page_table), ki], axis=1)
        gather = paged_gather(cache_dsa_k[l], cache_dsa_v[l], k_d, v_d, page_table)
        o_d = dsa_attend(q_d, qi, p['wi'], ki_all, pos_s, gather, topk_s)
        o_m = mem_attend(p['q_m'], cache_mem_k[l], cache_mem_v[l])
        x = merge(x, p, o_sb, o_d, o_m, w_up_sb[l], w_up_dsa[l], w_up_mem[l], w_out[l])
        s_sb_k.append(k_sb); s_sb_v.append(v_sb)
        s_dsa_k.append(k_d); s_dsa_v.append(v_d); s_idx_k.append(ki)
    y_sample = rmsnorm(x, final_norm_g)

    return (y_prompt, y_sample,
            jnp.stack(p_sb_k), jnp.stack(p_sb_v), jnp.stack(p_dsa_k), jnp.stack(p_dsa_v),
            jnp.stack(p_idx_k), jnp.stack(p_mem_k), jnp.stack(p_mem_v),
            jnp.stack(s_sb_k), jnp.stack(s_sb_v), jnp.stack(s_dsa_k), jnp.stack(s_dsa_v),
            jnp.stack(s_idx_k))
```

```python
import functools
import math

import numpy as np
import jax
import jax.numpy as jnp
from jax import lax
from jax.experimental import pallas as pl
from jax.experimental.pallas import tpu as pltpu

HEAD_DIM = 128
SB_HEADS = 6
DSA_HEADS = 6
MEM_HEADS = 4
SB_WIDTH = SB_HEADS * HEAD_DIM
DSA_WIDTH = DSA_HEADS * HEAD_DIM
MEM_WIDTH = MEM_HEADS * HEAD_DIM
IDX_HEADS = 16
IDX_DIM = 64
IDX_WIDTH = IDX_HEADS * IDX_DIM
PAGE_SIZE = 128
TOPK_MAX = 256
Q_BLOCK = 128
ROPE_THETA = 10000.0
NORM_EPS = 1e-6
ATTN_SCALE = HEAD_DIM ** -0.5
IDX_SCALE = (IDX_HEADS * IDX_DIM) ** -0.5

LANES = 128
NEG_BIAS = -1e30
INT32_MIN = -2 ** 31
KEY_NEG_INF = -2139095041
VMEM_LIMIT = 56 * 2 ** 20

_F32 = jnp.float32
_BF16 = jnp.bfloat16
_I32 = jnp.int32
_NT = (((1,), (1,)), ((), ()))


def _params(*sem):
    return pltpu.CompilerParams(dimension_semantics=sem, vmem_limit_bytes=VMEM_LIMIT)


def _sigmoid(x):
    return 1.0 / (1.0 + jnp.exp(-x))


def _silu(x):
    return x * _sigmoid(x)


def _softplus(s):
    return jnp.maximum(s, 0.0) + jnp.log(1.0 + jnp.exp(-jnp.abs(s)))


def _split_bf16(x):
    hi = x.astype(_BF16)
    lo = (x - hi.astype(_F32)).astype(_BF16)
    return hi, lo


def _sortable_key(s):
    b = lax.bitcast_convert_type(s, _I32)
    return b ^ ((b >> 31) & 0x7FFFFFFF)


def _largest_divisor(n, candidates):
    for c in candidates:
        if n % c == 0:
            return c
    raise ValueError(f"no tile in {candidates} divides {n}")


def _rms_kernel(x_ref, g_ref, o_ref):
    x = x_ref[...]
    y = x * lax.rsqrt(jnp.mean(x * x, axis=-1, keepdims=True) + NORM_EPS)
    o_ref[...] = (y * g_ref[...]).astype(o_ref.dtype)


def _rmsnorm(x, g, out_dtype):
    n, d = x.shape
    tm = _largest_divisor(n, (512, 256, 128, 64))
    return pl.pallas_call(
        _rms_kernel,
        grid=(n // tm,),
        in_specs=[pl.BlockSpec((tm, d), lambda i: (i, 0)),
                  pl.BlockSpec((1, d), lambda i: (0, 0))],
        out_specs=pl.BlockSpec((tm, d), lambda i: (i, 0)),
        out_shape=jax.ShapeDtypeStruct((n, d), out_dtype),
        compiler_params=_params("parallel"),
        name="rmsnorm",
    )(x, g.reshape(1, d))


def _proj_kernel(*refs, shifts, out_dtypes):
    h_ref, w_ref = refs[0], refs[1]
    n_tab = 1 + len(shifts) if shifts else 0
    tabs = refs[2:2 + n_tab]
    outs = refs[2 + n_tab:]
    z = jnp.dot(h_ref[...], w_ref[...], preferred_element_type=_F32)
    tn = z.shape[1]
    for g in range(tn // LANES):
        zg = z[:, g * LANES:(g + 1) * LANES]
        if shifts:
            r = zg * tabs[0][...]
            for t, sh in enumerate(shifts):
                r = r + pltpu.roll(zg, sh, 1) * tabs[1 + t][...]
            zg = r
        for o_ref in outs:
            o_ref[:, g * LANES:(g + 1) * LANES] = zg.astype(o_ref.dtype)


def _proj(h, w, layer, col0, width, out_dtypes, rope=None, name="proj"):
    n, d = h.shape
    tm = _largest_divisor(n, (1024, 512, 256, 128, 64))
    tn = _largest_divisor(width, (1024, 768, 512, 256, 128))
    assert col0 % tn == 0
    cb = col0 // tn
    shifts, tables = rope if rope else ((), ())
    in_specs = [pl.BlockSpec((tm, d), lambda j, i: (i, 0)),
                pl.BlockSpec((None, d, tn), lambda j, i: (layer, 0, cb + j))]
    for t in tables:
        nt = t.shape[0]
        assert nt % tm == 0 or tm % nt == 0
        if nt >= tm:
            nblk = nt // tm
            in_specs.append(pl.BlockSpec((tm, LANES), lambda j, i, nblk=nblk: (i % nblk, 0)))
        else:
            raise ValueError("rope table shorter than the row tile")
    out_specs = [pl.BlockSpec((tm, tn), lambda j, i: (i, j)) for _ in out_dtypes]
    out_shape = [jax.ShapeDtypeStruct((n, width), dt) for dt in out_dtypes]
    res = pl.pallas_call(
        functools.partial(_proj_kernel, shifts=tuple(shifts), out_dtypes=tuple(out_dtypes)),
        grid=(width // tn, n // tm),
        in_specs=in_specs,
        out_specs=out_specs,
        out_shape=out_shape,
        compiler_params=_params("parallel", "parallel"),
        name=name,
    )(h, w, *tables)
    return res


def _cumsum_matrix(t):
    r = np.arange(2 * t) % t
    c = np.arange(2 * t)
    m = np.where(c[None, :] < t, r[:, None] > c[None, :], True)
    return jnp.asarray(m, dtype=_BF16)


def _sb_block(q, kj, vj, u, carry, acc, before):
    t = kj.shape[0]
    s = lax.dot_general(q, kj, _NT, preferred_element_type=_F32) * ATTN_SCALE
    sp = _softplus(s)
    lk = -sp
    if before is not None:
        lk = jnp.where(before, lk, 0.0)
    hi, lo = _split_bf16(lk)
    cr = jnp.dot(jnp.concatenate([hi, lo], axis=1), u, preferred_element_type=_F32)
    a = jnp.exp(s - sp + carry + cr[:, :t])
    if before is not None:
        a = jnp.where(before, a, 0.0)
    acc = acc + jnp.dot(a.astype(_BF16), vj, preferred_element_type=_F32)
    return carry + cr[:, t:], acc


def _sb_prompt_kernel(q_ref, k_ref, v_ref, z_ref, u_ref, o_ref, *, tq):
    i = pl.program_id(2)
    q = q_ref[...]
    u = u_ref[...]
    row = lax.broadcasted_iota(_I32, (tq, tq), 0)
    col = lax.broadcasted_iota(_I32, (tq, tq), 1)
    zero = jnp.zeros((tq, tq), _F32)
    off = pl.multiple_of(i * tq, tq)
    carry, acc = _sb_block(q, k_ref[pl.ds(off, tq), :], v_ref[pl.ds(off, tq), :], u,
                           zero, jnp.zeros((tq, HEAD_DIM), _F32), col < row)

    def body(t, c):
        o = pl.multiple_of((i - 1 - t) * tq, tq)
        return _sb_block(q, k_ref[pl.ds(o, tq), :], v_ref[pl.ds(o, tq), :], u, c[0], c[1], None)

    carry, acc = lax.fori_loop(0, i, body, (carry, acc))
    o_ref[...] = (acc * _silu(z_ref[...])).astype(o_ref.dtype)


def _sb_prompt(q, k, v, z, b, t):
    tq = Q_BLOCK
    nq = t // tq
    u = _cumsum_matrix(tq)
    return pl.pallas_call(
        functools.partial(_sb_prompt_kernel, tq=tq),
        grid=(b, SB_HEADS, nq),
        in_specs=[pl.BlockSpec((tq, HEAD_DIM), lambda bi, h, i: (bi * nq + i, h)),
                  pl.BlockSpec((t, HEAD_DIM), lambda bi, h, i: (bi, h)),
                  pl.BlockSpec((t, HEAD_DIM), lambda bi, h, i: (bi, h)),
                  pl.BlockSpec((tq, HEAD_DIM), lambda bi, h, i: (bi * nq + i, h)),
                  pl.BlockSpec((2 * tq, 2 * tq), lambda bi, h, i: (0, 0))],
        out_specs=pl.BlockSpec((tq, HEAD_DIM), lambda bi, h, i: (bi * nq + i, h)),
        out_shape=jax.ShapeDtypeStruct(q.shape, _BF16),
        compiler_params=_params("parallel", "parallel", "arbitrary"),
        name="sb_prompt",
    )(q, k, v, z, u)


def _select_to_bias(key_ref, bias_ref, nch, ch, kk):
    rows = key_ref.shape[0]

    def count(pred):
        def body(c, part):
            off = pl.multiple_of(c * ch, ch)
            m = jnp.where(pred(key_ref[:, pl.ds(off, ch)], off), 1, 0).astype(_I32)
            for g in range(ch // LANES):
                part = part + m[:, g * LANES:(g + 1) * LANES]
            return part
        part = lax.fori_loop(0, nch, body, jnp.zeros((rows, LANES), _I32))
        return jnp.sum(part, axis=1, keepdims=True)

    def bit_body(it, u):
        cand_u = u | lax.shift_left(jnp.int32(1), 31 - it)
        cand = cand_u ^ jnp.int32(INT32_MIN)
        cnt = count(lambda kc, off: kc >= cand)
        return jnp.where(cnt >= kk, cand_u, u)

    u = lax.fori_loop(0, 32, bit_body, jnp.zeros((rows, 1), _I32))
    thr = u ^ jnp.int32(INT32_MIN)
    cnt_gt = count(lambda kc, off: kc > thr)
    cnt_ge = count(lambda kc, off: kc >= thr)
    need = kk - cnt_gt
    excess = jnp.where((cnt_ge > kk) & (thr > KEY_NEG_INF), 1, 0).astype(_I32)
    nbits = max(1, int(math.ceil(math.log2(key_ref.shape[1] + 1))))

    def tie_cut(_):
        def q_body(it, q):
            cand = q | lax.shift_left(jnp.int32(1), nbits - 1 - it)

            def pred(kc, off):
                idx = off + lax.broadcasted_iota(_I32, kc.shape, 1)
                return (kc == thr) & (idx < cand)
            return jnp.where(count(pred) < need, cand, q)
        return lax.fori_loop(0, nbits, q_body, jnp.zeros((rows, 1), _I32))

    def no_cut(_):
        return jnp.full((rows, 1), 2 ** 30, _I32)

    qcut = lax.cond(jnp.max(excess) > 0, tie_cut, no_cut, 0)

    def write(c, _):
        off = pl.multiple_of(c * ch, ch)
        kc = key_ref[:, pl.ds(off, ch)]
        idx = off + lax.broadcasted_iota(_I32, kc.shape, 1)
        sel = (kc > thr) | ((kc == thr) & (idx <= qcut))
        ok = sel & (kc > KEY_NEG_INF)
        bias_ref[:, pl.ds(off, ch)] = jnp.where(ok, 0.0, NEG_BIAS).astype(_F32)
        return 0

    lax.fori_loop(0, nch, write, 0)


def _idx_prompt_kernel(qi_ref, wi_ref, ki_ref, bias_ref, key_ref, *, tq, ch, kk):
    i = pl.program_id(1)
    nch = (i * tq + tq + ch - 1) // ch
    qi = qi_ref[...]
    wi = wi_ref[...]
    rowpos = i * tq + lax.broadcasted_iota(_I32, (tq, ch), 0)

    def score(c, _):
        off = pl.multiple_of(c * ch, ch)
        kc = ki_ref[pl.ds(off, ch), :]
        acc = jnp.zeros((tq, ch), _F32)
        for h in range(IDX_HEADS):
            r = lax.dot_general(qi[:, h * IDX_DIM:(h + 1) * IDX_DIM], kc, _NT,
                                preferred_element_type=_F32)
            acc = acc + jnp.maximum(r, 0.0) * wi[:, IDX_DIM + h:IDX_DIM + h + 1]
        sc = acc * IDX_SCALE
        sc = jnp.where(sc == 0.0, 0.0, sc)
        colpos = off + lax.broadcasted_iota(_I32, (tq, ch), 1)
        sc = jnp.where(colpos <= rowpos, sc, -jnp.inf)
        key_ref[:, pl.ds(off, ch)] = _sortable_key(sc)
        return 0

    lax.fori_loop(0, nch, score, 0)
    bias_ref[...] = jnp.full(bias_ref.shape, NEG_BIAS, _F32)
    _select_to_bias(key_ref, bias_ref, nch, ch, kk)


def _idx_prompt(qi, kiwi, ki, b, t, kk):
    tq = Q_BLOCK
    nq = t // tq
    ch = _largest_divisor(t, (512, 256, 128))
    return pl.pallas_call(
        functools.partial(_idx_prompt_kernel, tq=tq, ch=ch, kk=kk),
        grid=(b, nq),
        in_specs=[pl.BlockSpec((tq, IDX_WIDTH), lambda bi, i: (bi * nq + i, 0)),
                  pl.BlockSpec((tq, LANES), lambda bi, i: (bi * nq + i, 0)),
                  pl.BlockSpec((t, IDX_DIM), lambda bi, i: (bi, 0))],
        out_specs=pl.BlockSpec((tq, t), lambda bi, i: (bi * nq + i, 0)),
        out_shape=jax.ShapeDtypeStruct((b * t, t), _F32),
        scratch_shapes=[pltpu.VMEM((tq, t), _I32)],
        compiler_params=_params("parallel", "arbitrary"),
        name="idx_select_prompt",
    )(qi, kiwi, ki)


def _dsa_prompt_kernel(q_ref, k_ref, v_ref, bias_ref, z_ref, o_ref, *, tq, ch):
    i = pl.program_id(1)
    nch = (i * tq + tq + ch - 1) // ch
    for h in range(DSA_HEADS):
        hs = slice(h * HEAD_DIM, (h + 1) * HEAD_DIM)
        qh = q_ref[:, hs]

        def body(c, carry, hs=hs, qh=qh):
            m, l, acc = carry
            off = pl.multiple_of(c * ch, ch)
            s = lax.dot_general(qh, k_ref[pl.ds(off, ch), hs], _NT, preferred_element_type=_F32)
            s = s * ATTN_SCALE + bias_ref[:, pl.ds(off, ch)]
            m_new = jnp.maximum(m, jnp.max(s, axis=1, keepdims=True))
            alpha = jnp.exp(m - m_new)
            p = jnp.exp(s - m_new)
            l = l * alpha + jnp.sum(p, axis=1, keepdims=True)
            acc = acc * alpha + jnp.dot(p.astype(_BF16), v_ref[pl.ds(off, ch), hs],
                                        preferred_element_type=_F32)
            return m_new, l, acc

        m, l, acc = lax.fori_loop(
            0, nch, body,
            (jnp.full((tq, 1), NEG_BIAS, _F32), jnp.zeros((tq, 1), _F32), jnp.zeros((tq, HEAD_DIM), _F32)))
        o_ref[:, hs] = (acc / l * _silu(z_ref[:, hs])).astype(o_ref.dtype)


def _dsa_prompt(q, k, v, bias, z, b, t):
    tq = Q_BLOCK
    nq = t // tq
    ch = _largest_divisor(t, (512, 256, 128))
    w = DSA_WIDTH
    return pl.pallas_call(
        functools.partial(_dsa_prompt_kernel, tq=tq, ch=ch),
        grid=(b, nq),
        in_specs=[pl.BlockSpec((tq, w), lambda bi, i: (bi * nq + i, 0)),
                  pl.BlockSpec((t, w), lambda bi, i: (bi, 0)),
                  pl.BlockSpec((t, w), lambda bi, i: (bi, 0)),
                  pl.BlockSpec((tq, t), lambda bi, i: (bi * nq + i, 0)),
                  pl.BlockSpec((tq, w), lambda bi, i: (bi * nq + i, 0))],
        out_specs=pl.BlockSpec((tq, w), lambda bi, i: (bi * nq + i, 0)),
        out_shape=jax.ShapeDtypeStruct(q.shape, _BF16),
        compiler_params=_params("parallel", "arbitrary"),
        name="dsa_prompt",
    )(q, k, v, bias, z)


def _mem_kernel(q_ref, mk_ref, mv_ref, z_ref, o_ref):
    tm = q_ref.shape[0]
    pad = (-tm) % 16
    for h in range(MEM_HEADS):
        hs = slice(h * HEAD_DIM, (h + 1) * HEAD_DIM)
        qh = q_ref[:, hs].astype(_F32)
        if pad:
            qh = jnp.concatenate([qh, jnp.zeros((pad, HEAD_DIM), _F32)], axis=0)
        s = lax.dot_general(qh.astype(_BF16), mk_ref[:, hs].astype(_BF16), _NT,
                            preferred_element_type=_F32) * ATTN_SCALE
        p = jnp.exp(s - jnp.max(s, axis=1, keepdims=True))
        p = p / jnp.sum(p, axis=1, keepdims=True)
        o = jnp.dot(p.astype(_BF16), mv_ref[:, hs].astype(_BF16), preferred_element_type=_F32)
        o_ref[:, hs] = (o[:tm] * _silu(z_ref[:, hs])).astype(o_ref.dtype)


def _mem_attend(q, mk, mv, z):
    g, r, w = q.shape
    m = mk.shape[1]
    tm = _largest_divisor(r, (512, 256, 128, 64, 8))
    return pl.pallas_call(
        _mem_kernel,
        grid=(g, r // tm),
        in_specs=[pl.BlockSpec((None, tm, w), lambda gi, i: (gi, i, 0)),
                  pl.BlockSpec((None, m, w), lambda gi, i: (gi, 0, 0)),
                  pl.BlockSpec((None, m, w), lambda gi, i: (gi, 0, 0)),
                  pl.BlockSpec((None, tm, w), lambda gi, i: (gi, i, 0))],
        out_specs=pl.BlockSpec((None, tm, w), lambda gi, i: (gi, i, 0)),
        out_shape=jax.ShapeDtypeStruct((g, r, w), _BF16),
        compiler_params=_params("parallel", "parallel"),
        name="mem_attend",
    )(q, mk, mv, z)


def _mix_kernel(gs_ref, gd_ref, gm_ref, g1_ref, g2_ref, g3_ref, w1_ref, w2_ref, w3_ref, o_ref):
    a = jnp.dot(gs_ref[...], w1_ref[...], preferred_element_type=_F32)
    c = jnp.dot(gd_ref[...], w2_ref[...], preferred_element_type=_F32)
    m = jnp.dot(gm_ref[...], w3_ref[...], preferred_element_type=_F32)
    mixed = _sigmoid(g1_ref[...]) * a + _sigmoid(g2_ref[...]) * c + _sigmoid(g3_ref[...]) * m
    o_ref[...] = mixed.astype(o_ref.dtype)


def _mix(gs, gd, gm, gates, w1, w2, w3, layer):
    n = gs.shape[0]
    d = w1.shape[2]
    tm = _largest_divisor(n, (256, 128, 64))
    row = lambda w: pl.BlockSpec((tm, w), lambda i: (i, 0))
    wspec = lambda w: pl.BlockSpec((None, w.shape[1], d), lambda i: (layer, 0, 0))
    return pl.pallas_call(
        _mix_kernel,
        grid=(n // tm,),
        in_specs=[row(SB_WIDTH), row(DSA_WIDTH), row(MEM_WIDTH),
                  pl.BlockSpec((tm, d), lambda i: (i, 0)),
                  pl.BlockSpec((tm, d), lambda i: (i, 1)),
                  pl.BlockSpec((tm, d), lambda i: (i, 2)),
                  wspec(w1), wspec(w2), wspec(w3)],
        out_specs=pl.BlockSpec((tm, d), lambda i: (i, 0)),
        out_shape=jax.ShapeDtypeStruct((n, d), _BF16),
        compiler_params=_params("parallel"),
        name="mix",
    )(gs, gd, gm, gates, gates, gates, w1, w2, w3)


def _out_kernel(x_ref, m_ref, w_ref, g_ref, xo_ref, ho_ref):
    xn = x_ref[...] + jnp.dot(m_ref[...], w_ref[...], preferred_element_type=_F32)
    xo_ref[...] = xn
    y = xn * lax.rsqrt(jnp.mean(xn * xn, axis=-1, keepdims=True) + NORM_EPS)
    ho_ref[...] = (y * g_ref[...]).astype(ho_ref.dtype)


def _out_proj(x, mixed, w, layer, g, h_dtype):
    n, d = x.shape
    tm = _largest_divisor(n, (512, 256, 128, 64))
    return pl.pallas_call(
        _out_kernel,
        grid=(n // tm,),
        in_specs=[pl.BlockSpec((tm, d), lambda i: (i, 0)),
                  pl.BlockSpec((tm, d), lambda i: (i, 0)),
                  pl.BlockSpec((None, d, d), lambda i: (layer, 0, 0)),
                  pl.BlockSpec((1, d), lambda i: (0, 0))],
        out_specs=[pl.BlockSpec((tm, d), lambda i: (i, 0)),
                   pl.BlockSpec((tm, d), lambda i: (i, 0))],
        out_shape=[jax.ShapeDtypeStruct((n, d), _F32), jax.ShapeDtypeStruct((n, d), h_dtype)],
        compiler_params=_params("parallel"),
        name="out_proj",
    )(x, mixed, w, g.reshape(1, d))


def _head_rows(x, ds, h):
    return x[h * ds:(h + 1) * ds]


def _stack_scores(qs, kp, ds, heads):
    parts = []
    for h in range(heads):
        r = lax.dot_general(qs, kp[:, h * HEAD_DIM:(h + 1) * HEAD_DIM], _NT, preferred_element_type=_F32)
        parts.append(_head_rows(r, ds, h))
    return jnp.concatenate(parts, axis=0) * ATTN_SCALE


def _stack_pv(p, vp, ds, heads):
    pb = p.astype(_BF16)
    parts = []
    for h in range(heads):
        r = jnp.dot(pb, vp[:, h * HEAD_DIM:(h + 1) * HEAD_DIM], preferred_element_type=_F32)
        parts.append(_head_rows(r, ds, h))
    return jnp.concatenate(parts, axis=0)


def _pad_rows(x, rows):
    return jnp.concatenate([x, jnp.zeros((rows - x.shape[0], x.shape[1]), x.dtype)], axis=0)


def _sb_sample_kernel(pt_ref, q_ref, kn_ref, vn_ref, kc_ref, vc_ref, z_ref, u_ref, o_ref,
                      carry_ref, acc_ref, *, ds, n_pages):
    p = pl.program_id(1)
    rows = SB_HEADS * ds
    qs = q_ref[...]
    u = u_ref[...]

    def step(kp, vp, before):
        s = _stack_scores(qs, kp, ds, SB_HEADS)
        sp = _softplus(s)
        lk = -sp
        if before is not None:
            lk = jnp.where(before, lk, 0.0)
        hi, lo = _split_bf16(lk)
        cr = jnp.dot(jnp.concatenate([hi, lo], axis=1), u, preferred_element_type=_F32)
        a = jnp.exp(s - sp + carry_ref[...] + cr[:, :PAGE_SIZE])
        if before is not None:
            a = jnp.where(before, a, 0.0)
        acc_ref[...] += _stack_pv(a, vp, ds, SB_HEADS)
        carry_ref[...] += cr[:, PAGE_SIZE:]

    @pl.when(p == 0)
    def _():
        carry_ref[...] = jnp.zeros_like(carry_ref)
        acc_ref[...] = jnp.zeros_like(acc_ref)
        kn = _pad_rows(kn_ref[...], PAGE_SIZE).astype(_BF16)
        vn = _pad_rows(vn_ref[...], PAGE_SIZE).astype(_BF16)
        qrow = lax.broadcasted_iota(_I32, (rows, PAGE_SIZE), 0) % ds
        col = lax.broadcasted_iota(_I32, (rows, PAGE_SIZE), 1)
        step(kn, vn, col < qrow)

    step(kc_ref[...].astype(_BF16), vc_ref[...].astype(_BF16), None)

    @pl.when(p == n_pages - 1)
    def _():
        o_ref[...] = (acc_ref[...] * _silu(z_ref[...])).astype(o_ref.dtype)


def _sb_sample(qs, kn, vn, cache_k, cache_v, zs, page_table, layer):
    db, rows, _ = qs.shape
    ds = rows // SB_HEADS
    n_pages = page_table.shape[1]
    w = SB_WIDTH
    u = _cumsum_matrix(PAGE_SIZE)
    seq = lambda r, c: pl.BlockSpec((None, r, c), lambda b, p, pt: (b, 0, 0))
    page = pl.BlockSpec((None, None, PAGE_SIZE, w),
                        lambda b, p, pt: (layer, pt[b, n_pages - 1 - p], 0, 0))
    return pl.pallas_call(
        functools.partial(_sb_sample_kernel, ds=ds, n_pages=n_pages),
        grid_spec=pltpu.PrefetchScalarGridSpec(
            num_scalar_prefetch=1,
            grid=(db, n_pages),
            in_specs=[seq(rows, HEAD_DIM), seq(ds, w), seq(ds, w), page, page, seq(rows, HEAD_DIM),
                      pl.BlockSpec((2 * PAGE_SIZE, 2 * PAGE_SIZE), lambda b, p, pt: (0, 0))],
            out_specs=seq(rows, HEAD_DIM),
            scratch_shapes=[pltpu.VMEM((rows, PAGE_SIZE), _F32), pltpu.VMEM((rows, HEAD_DIM), _F32)]),
        out_shape=jax.ShapeDtypeStruct(qs.shape, _BF16),
        compiler_params=_params("parallel", "arbitrary"),
        name="sb_sample",
    )(page_table, qs, kn, vn, cache_k, cache_v, zs, u)


def _idx_sample_kernel(pt_ref, qi_ref, wi_ref, kn_ref, kc_ref, o_ref, *, ds, n_pages):
    p = pl.program_id(1)

    def scores(kp):
        r = lax.dot_general(qi_ref[...], kp, _NT, preferred_element_type=_F32)
        r = jnp.maximum(r, 0.0) * wi_ref[...]
        sc = _head_rows(r, ds, 0)
        for h in range(1, IDX_HEADS):
            sc = sc + _head_rows(r, ds, h)
        sc = sc * IDX_SCALE
        return jnp.where(sc == 0.0, 0.0, sc)

    @pl.when(p < n_pages)
    def _():
        o_ref[...] = _sortable_key(scores(kc_ref[...].astype(_BF16)))

    @pl.when(p == n_pages)
    def _():
        sc = scores(_pad_rows(kn_ref[...], PAGE_SIZE).astype(_BF16))
        s_row = lax.broadcasted_iota(_I32, (ds, PAGE_SIZE), 0)
        col = lax.broadcasted_iota(_I32, (ds, PAGE_SIZE), 1)
        o_ref[...] = _sortable_key(jnp.where(col <= s_row, sc, -jnp.inf))


def _idx_sample(qi_t, wi_col, ki_new, cache_ik, page_table, layer):
    db, rows, _ = qi_t.shape
    ds = rows // IDX_HEADS
    n_pages = page_table.shape[1]
    seq = lambda r, c: pl.BlockSpec((None, r, c), lambda b, p, pt: (b, 0, 0))
    page = pl.BlockSpec((None, None, PAGE_SIZE, IDX_DIM),
                        lambda b, p, pt: (layer, pt[b, jnp.minimum(p, n_pages - 1)], 0, 0))
    return pl.pallas_call(
        functools.partial(_idx_sample_kernel, ds=ds, n_pages=n_pages),
        grid_spec=pltpu.PrefetchScalarGridSpec(
            num_scalar_prefetch=1,
            grid=(db, n_pages + 1),
            in_specs=[seq(rows, IDX_DIM), seq(rows, 1), seq(ds, IDX_DIM), page],
            out_specs=pl.BlockSpec((None, ds, PAGE_SIZE), lambda b, p, pt: (b, 0, p))),
        out_shape=jax.ShapeDtypeStruct((db, ds, (n_pages + 1) * PAGE_SIZE), _I32),
        compiler_params=_params("parallel", "arbitrary"),
        name="idx_scores_sample",
    )(page_table, qi_t, wi_col, ki_new, cache_ik)


def _select_kernel(key_ref, bias_ref, *, ch, kk):
    _select_to_bias(key_ref, bias_ref, key_ref.shape[1] // ch, ch, kk)


def _select_sample(keys, kk):
    rows, s = keys.shape
    ch = _largest_divisor(s, (512, 384, 256, 128))
    return pl.pallas_call(
        functools.partial(_select_kernel, ch=ch, kk=kk),
        grid=(1,),
        in_specs=[pl.BlockSpec((rows, s), lambda i: (0, 0))],
        out_specs=pl.BlockSpec((rows, s), lambda i: (0, 0)),
        out_shape=jax.ShapeDtypeStruct((rows, s), _F32),
        compiler_params=_params("arbitrary"),
        name="select_sample",
    )(keys)


def _dsa_sample_kernel(pt_ref, q_ref, kn_ref, vn_ref, kc_ref, vc_ref, bn_ref, bc_ref, z_ref, o_ref,
                       m_ref, l_ref, acc_ref, *, ds, n_pages):
    p = pl.program_id(1)
    qs = q_ref[...]

    def step(kp, vp, bias):
        s = _stack_scores(qs, kp, ds, DSA_HEADS) + jnp.concatenate([bias] * DSA_HEADS, axis=0)
        m_old = m_ref[...]
        m_new = jnp.maximum(m_old, jnp.max(s, axis=1, keepdims=True))
        alpha = jnp.exp(m_old - m_new)
        pr = jnp.exp(s - m_new)
        l_ref[...] = l_ref[...] * alpha + jnp.sum(pr, axis=1, keepdims=True)
        acc_ref[...] = acc_ref[...] * alpha + _stack_pv(pr, vp, ds, DSA_HEADS)
        m_ref[...] = m_new

    @pl.when(p == 0)
    def _():
        m_ref[...] = jnp.full(m_ref.shape, NEG_BIAS, _F32)
        l_ref[...] = jnp.zeros_like(l_ref)
        acc_ref[...] = jnp.zeros_like(acc_ref)
        step(_pad_rows(kn_ref[...], PAGE_SIZE).astype(_BF16),
             _pad_rows(vn_ref[...], PAGE_SIZE).astype(_BF16), bn_ref[...])

    step(kc_ref[...].astype(_BF16), vc_ref[...].astype(_BF16), bc_ref[...])

    @pl.when(p == n_pages - 1)
    def _():
        o_ref[...] = (acc_ref[...] / l_ref[...] * _silu(z_ref[...])).astype(o_ref.dtype)


def _dsa_sample(qs, kn, vn, cache_k, cache_v, bias, zs, page_table, layer):
    db, rows, _ = qs.shape
    ds = rows // DSA_HEADS
    n_pages = page_table.shape[1]
    w = DSA_WIDTH
    seq = lambda r, c: pl.BlockSpec((None, r, c), lambda b, p, pt: (b, 0, 0))
    page = pl.BlockSpec((None, None, PAGE_SIZE, w), lambda b, p, pt: (layer, pt[b, p], 0, 0))
    return pl.pallas_call(
        functools.partial(_dsa_sample_kernel, ds=ds, n_pages=n_pages),
        grid_spec=pltpu.PrefetchScalarGridSpec(
            num_scalar_prefetch=1,
            grid=(db, n_pages),
            in_specs=[seq(rows, HEAD_DIM), seq(ds, w), seq(ds, w), page, page,
                      pl.BlockSpec((None, ds, PAGE_SIZE), lambda b, p, pt: (b, 0, n_pages)),
                      pl.BlockSpec((None, ds, PAGE_SIZE), lambda b, p, pt: (b, 0, p)),
                      seq(rows, HEAD_DIM)],
            out_specs=seq(rows, HEAD_DIM),
            scratch_shapes=[pltpu.VMEM((rows, 1), _F32), pltpu.VMEM((rows, 1), _F32),
                            pltpu.VMEM((rows, HEAD_DIM), _F32)]),
        out_shape=jax.ShapeDtypeStruct(qs.shape, _BF16),
        compiler_params=_params("parallel", "arbitrary"),
        name="dsa_sample",
    )(page_table, qs, kn, vn, cache_k, cache_v, bias, bias, zs)


def _rope_tables(pos, reps):
    def base(half):
        inv_freq = jnp.power(jnp.float32(ROPE_THETA), -jnp.arange(half, dtype=_F32) / half)
        ang = pos.astype(_F32)[:, None] * inv_freq[None, :]
        return jnp.cos(ang), jnp.sin(ang)

    tile = lambda a: jnp.tile(a, (reps, 1))
    c, s = base(HEAD_DIM // 2)
    head = (tile(jnp.concatenate([c, c], axis=1)), tile(jnp.concatenate([-s, s], axis=1)))
    c, s = base(IDX_DIM // 2)
    z = jnp.zeros_like(s)
    c64, lo64, hi64 = (jnp.concatenate([c, c], axis=1), jnp.concatenate([-s, z], axis=1),
                       jnp.concatenate([z, s], axis=1))
    idx = tuple(tile(jnp.concatenate([a, a], axis=1)) for a in (c64, lo64, hi64))
    one, zero = jnp.ones_like(c64), jnp.zeros_like(c64)
    kiwi = (tile(jnp.concatenate([c64, one], axis=1)), tile(jnp.concatenate([lo64, zero], axis=1)),
            tile(jnp.concatenate([hi64, zero], axis=1)))
    return head, idx, kiwi


_ROPE128 = (LANES // 2,)
_ROPE64 = (LANES - IDX_DIM // 2, IDX_DIM // 2)

_OFF = dict(q_sb=0, k_sb=768, v_sb=1536, z_sb=2304, q_d=3072, k_d=3840, v_d=4608, z_d=5376, qi=6144)
_KIWI0 = 7168
_TAIL0 = _KIWI0 + IDX_DIM + IDX_HEADS


def _project_all(h, w_main, w_kiwi, w_tail, layer, tabs, d_model):
    head_t, idx_t, kiwi_t = tabs
    pj = functools.partial(_proj, h)
    o = {}
    (o["q_sb"],) = pj(w_main, layer, _OFF["q_sb"], SB_WIDTH, (_BF16,), name="proj_q_sb")
    o["k_sb"], o["k_sb_b"] = pj(w_main, layer, _OFF["k_sb"], SB_WIDTH, (_F32, _BF16), name="proj_k_sb")
    o["v_sb"], o["v_sb_b"] = pj(w_main, layer, _OFF["v_sb"], SB_WIDTH, (_F32, _BF16), name="proj_v_sb")
    (o["z_sb"],) = pj(w_main, layer, _OFF["z_sb"], SB_WIDTH, (_F32,), name="proj_z_sb")
    (o["q_d"],) = pj(w_main, layer, _OFF["q_d"], DSA_WIDTH, (_BF16,), rope=(_ROPE128, head_t), name="proj_q_d")
    o["k_d"], o["k_d_b"] = pj(w_main, layer, _OFF["k_d"], DSA_WIDTH, (_F32, _BF16), rope=(_ROPE128, head_t),
                              name="proj_k_d")
    o["v_d"], o["v_d_b"] = pj(w_main, layer, _OFF["v_d"], DSA_WIDTH, (_F32, _BF16), name="proj_v_d")
    (o["z_d"],) = pj(w_main, layer, _OFF["z_d"], DSA_WIDTH, (_F32,), name="proj_z_d")
    (o["qi"],) = pj(w_main, layer, _OFF["qi"], IDX_WIDTH, (_BF16,), rope=(_ROPE64, idx_t), name="proj_qi")
    (o["kiwi"],) = pj(w_kiwi, layer, 0, LANES, (_F32,), rope=(_ROPE64, kiwi_t), name="proj_kiwi")
    (o["q_m"],) = pj(w_tail, layer, 0, MEM_WIDTH, (_F32,), name="proj_q_m")
    (o["z_m"],) = pj(w_tail, layer, MEM_WIDTH, MEM_WIDTH, (_F32,), name="proj_z_m")
    (o["gates"],) = pj(w_tail, layer, 2 * MEM_WIDTH, 3 * d_model, (_F32,), name="proj_gates")
    return o


def kernel(x_prompt, x_sample, cache_sb_k, cache_sb_v, cache_dsa_k, cache_dsa_v, cache_idx_k, cache_mem_k,
           cache_mem_v, page_table, mem_prompt, norm_g, w_in, w_up_sb, w_up_dsa, w_up_mem, w_out, w_mem_k,
           w_mem_v, final_norm_g):
    b, t, d = x_prompt.shape
    db, ds, _ = x_sample.shape
    depth = w_in.shape[0]
    n_pool = cache_sb_k.shape[1]
    n_pages = page_table.shape[1]
    n_past = n_pages * PAGE_SIZE
    mem_len = mem_prompt.shape[1]
    topk_p = min(TOPK_MAX, t // 4)
    topk_s = min(TOPK_MAX, (n_past + ds) // 4)
    assert t % Q_BLOCK == 0 and w_in.shape[2] == _TAIL0 + 2 * MEM_WIDTH + 3 * d

    w_main = w_in[:, :, :_KIWI0].astype(_BF16)
    w_kiwi = jnp.pad(w_in[:, :, _KIWI0:_TAIL0], ((0, 0), (0, 0), (0, LANES - (_TAIL0 - _KIWI0)))).astype(_BF16)
    w_tail = w_in[:, :, _TAIL0:].astype(_BF16)
    w_up_sb_b, w_up_dsa_b, w_up_mem_b = (w.astype(_BF16) for w in (w_up_sb, w_up_dsa, w_up_mem))
    w_out_b = w_out.astype(_BF16)
    w_mem_kv = jnp.concatenate([w_mem_k, w_mem_v], axis=2).astype(_BF16)
    gains = jnp.concatenate([norm_g, final_norm_g[None]], axis=0)

    tabs_p = _rope_tables(jnp.arange(t), b)
    tabs_s = _rope_tables(n_past + jnp.arange(ds), db)

    n = b * t
    x = x_prompt.reshape(n, d)
    h = _rmsnorm(x, gains[0], _BF16)
    mem_b = mem_prompt.reshape(b * mem_len, d).astype(_BF16)
    outs_p = {k: [] for k in ("k_sb", "v_sb", "k_d", "v_d", "ki", "mk", "mv")}
    for l in range(depth):
        o = _project_all(h, w_main, w_kiwi, w_tail, l, tabs_p, d)
        ki = o["kiwi"][:, :IDX_DIM]
        g_sb = _sb_prompt(o["q_sb"], o["k_sb_b"], o["v_sb_b"], o["z_sb"], b, t)
        bias = _idx_prompt(o["qi"], o["kiwi"], ki.astype(_BF16), b, t, topk_p)
        g_d = _dsa_prompt(o["q_d"], o["k_d_b"], o["v_d_b"], bias, o["z_d"], b, t)
        (mkv,) = _proj(mem_b, w_mem_kv, l, 0, 2 * MEM_WIDTH, (_F32,), name="proj_mem_kv")
        mk, mv = mkv[:, :MEM_WIDTH], mkv[:, MEM_WIDTH:]
        g_m = _mem_attend(o["q_m"].reshape(b, t, MEM_WIDTH), mk.reshape(b, mem_len, MEM_WIDTH),
                          mv.reshape(b, mem_len, MEM_WIDTH), o["z_m"].reshape(b, t, MEM_WIDTH))
        mixed = _mix(g_sb, g_d, g_m.reshape(n, MEM_WIDTH), o["gates"], w_up_sb_b, w_up_dsa_b, w_up_mem_b, l)
        last = l == depth - 1
        x, h = _out_proj(x, mixed, w_out_b, l, gains[l + 1], _F32 if last else _BF16)
        for name, val in (("k_sb", o["k_sb"]), ("v_sb", o["v_sb"]), ("k_d", o["k_d"]), ("v_d", o["v_d"]),
                          ("ki", ki), ("mk", mk), ("mv", mv)):
            outs_p[name].append(val)
    y_prompt = h.reshape(b, t, d)

    ns = db * ds
    x = x_sample.reshape(ns, d)
    h = _rmsnorm(x, gains[0], _BF16)
    c_sb_k = cache_sb_k.reshape(depth, n_pool, PAGE_SIZE, SB_WIDTH)
    c_sb_v = cache_sb_v.reshape(depth, n_pool, PAGE_SIZE, SB_WIDTH)
    c_d_k = cache_dsa_k.reshape(depth, n_pool, PAGE_SIZE, DSA_WIDTH)
    c_d_v = cache_dsa_v.reshape(depth, n_pool, PAGE_SIZE, DSA_WIDTH)
    c_mem_k = cache_mem_k.reshape(depth, db, mem_len, MEM_WIDTH)
    c_mem_v = cache_mem_v.reshape(depth, db, mem_len, MEM_WIDTH)

    def stack(a, heads):
        c = a.shape[1] // heads
        return a.reshape(db, ds, heads, c).transpose(0, 2, 1, 3).reshape(db, heads * ds, c)

    def unstack(a, heads):
        c = a.shape[2]
        return a.reshape(db, heads, ds, c).transpose(0, 2, 1, 3).reshape(db * ds, heads * c)

    outs_s = {k: [] for k in ("k_sb", "v_sb", "k_d", "v_d", "ki")}
    for l in range(depth):
        o = _project_all(h, w_main, w_kiwi, w_tail, l, tabs_s, d)
        ki = o["kiwi"][:, :IDX_DIM]
        wi = o["kiwi"][:, IDX_DIM:IDX_DIM + IDX_HEADS]
        g_sb = _sb_sample(stack(o["q_sb"], SB_HEADS), o["k_sb"].reshape(db, ds, SB_WIDTH),
                          o["v_sb"].reshape(db, ds, SB_WIDTH), c_sb_k, c_sb_v, stack(o["z_sb"], SB_HEADS),
                          page_table, l)
        keys = _idx_sample(stack(o["qi"], IDX_HEADS), stack(wi, IDX_HEADS), ki.reshape(db, ds, IDX_DIM),
                           cache_idx_k, page_table, l)
        bias = _select_sample(keys.reshape(ns, -1), topk_s).reshape(db, ds, -1)
        g_d = _dsa_sample(stack(o["q_d"], DSA_HEADS), o["k_d"].reshape(db, ds, DSA_WIDTH),
                          o["v_d"].reshape(db, ds, DSA_WIDTH), c_d_k, c_d_v, bias, stack(o["z_d"], DSA_HEADS),
                          page_table, l)
        g_m = _mem_attend(o["q_m"].reshape(db, ds, MEM_WIDTH), c_mem_k[l], c_mem_v[l],
                          o["z_m"].reshape(db, ds, MEM_WIDTH))
        mixed = _mix(unstack(g_sb, SB_HEADS), unstack(g_d, DSA_HEADS), g_m.reshape(ns, MEM_WIDTH), o["gates"],
                     w_up_sb_b, w_up_dsa_b, w_up_mem_b, l)
        last = l == depth - 1
        x, h = _out_proj(x, mixed, w_out_b, l, gains[l + 1], _F32 if last else _BF16)
        for name in ("k_sb", "v_sb", "k_d", "v_d"):
            outs_s[name].append(o[name])
        outs_s["ki"].append(ki)
    y_sample = h.reshape(db, ds, d)

    hp = lambda xs, heads: jnp.stack(xs).reshape(depth, b, -1, heads, HEAD_DIM)
    hs = lambda xs, heads: jnp.stack(xs).reshape(depth, db, ds, heads, HEAD_DIM)
    return (y_prompt, y_sample,
            hp(outs_p["k_sb"], SB_HEADS), hp(outs_p["v_sb"], SB_HEADS),
            hp(outs_p["k_d"], DSA_HEADS), hp(outs_p["v_d"], DSA_HEADS),
            jnp.stack(outs_p["ki"]).reshape(depth, b, t, IDX_DIM),
            hp(outs_p["mk"], MEM_HEADS), hp(outs_p["mv"], MEM_HEADS),
            hs(outs_s["k_sb"], SB_HEADS), hs(outs_s["v_sb"], SB_HEADS),
            hs(outs_s["k_d"], DSA_HEADS), hs(outs_s["v_d"], DSA_HEADS),
            jnp.stack(outs_s["ki"]).reshape(depth, db, ds, IDX_DIM))
```

```python
import functools
import math

import numpy as np
import jax
import jax.numpy as jnp
from jax import lax
from jax.experimental import pallas as pl
from jax.experimental.pallas import tpu as pltpu

HEAD_DIM = 128
SB_HEADS = 6
DSA_HEADS = 6
MEM_HEADS = 4
SB_WIDTH = SB_HEADS * HEAD_DIM
DSA_WIDTH = DSA_HEADS * HEAD_DIM
MEM_WIDTH = MEM_HEADS * HEAD_DIM
IDX_HEADS = 16
IDX_DIM = 64
IDX_WIDTH = IDX_HEADS * IDX_DIM
PAGE_SIZE = 128
TOPK_MAX = 256
Q_BLOCK = 128
ROPE_THETA = 10000.0
NORM_EPS = 1e-6
ATTN_SCALE = HEAD_DIM ** -0.5
IDX_SCALE = (IDX_HEADS * IDX_DIM) ** -0.5

LANES = 128
NEG_BIAS = -1e30
INT32_MIN = -2 ** 31
KEY_NEG_INF = -2139095041
VMEM_LIMIT = 56 * 2 ** 20

_F32 = jnp.float32
_BF16 = jnp.bfloat16
_I32 = jnp.int32
_NT = (((1,), (1,)), ((), ()))


def _params(*sem):
    return pltpu.CompilerParams(dimension_semantics=sem, vmem_limit_bytes=VMEM_LIMIT)


def _sigmoid(x):
    return 1.0 / (1.0 + jnp.exp(-x))


def _silu(x):
    return x * _sigmoid(x)


def _softplus(s):
    return jnp.maximum(s, 0.0) + jnp.log(1.0 + jnp.exp(-jnp.abs(s)))


def _split_bf16(x):
    hi = x.astype(_BF16)
    lo = (x - hi.astype(_F32)).astype(_BF16)
    return hi, lo


def _sortable_key(s):
    b = lax.bitcast_convert_type(s, _I32)
    return b ^ ((b >> 31) & 0x7FFFFFFF)


def _largest_divisor(n, candidates):
    for c in candidates:
        if n % c == 0:
            return c
    raise ValueError(f"no tile in {candidates} divides {n}")


def _rms_kernel(x_ref, g_ref, o_ref):
    x = x_ref[...]
    y = x * lax.rsqrt(jnp.mean(x * x, axis=-1, keepdims=True) + NORM_EPS)
    o_ref[...] = (y * g_ref[...]).astype(o_ref.dtype)


def _rmsnorm(x, g, out_dtype):
    n, d = x.shape
    tm = _largest_divisor(n, (512, 256, 128, 64))
    return pl.pallas_call(
        _rms_kernel,
        grid=(n // tm,),
        in_specs=[pl.BlockSpec((tm, d), lambda i: (i, 0)),
                  pl.BlockSpec((1, d), lambda i: (0, 0))],
        out_specs=pl.BlockSpec((tm, d), lambda i: (i, 0)),
        out_shape=jax.ShapeDtypeStruct((n, d), out_dtype),
        compiler_params=_params("parallel"),
        name="rmsnorm",
    )(x, g.reshape(1, d))


def _proj_kernel(*refs, shifts, kinds, n_alias):
    h_ref, w_ref = refs[0], refs[1]
    n_tab = 1 + len(shifts) if shifts else 0
    tabs = refs[2:2 + n_tab]
    outs = refs[2 + n_tab + n_alias:]
    z = jnp.dot(h_ref[...], w_ref[...], preferred_element_type=_F32)
    tn = z.shape[1]
    for g in range(tn // LANES):
        zg = z[:, g * LANES:(g + 1) * LANES]
        if shifts:
            r = zg * tabs[0][...]
            for t, sh in enumerate(shifts):
                r = r + pltpu.roll(zg, sh, 1) * tabs[1 + t][...]
            zg = r
        for kind, o_ref in zip(kinds, outs):
            if kind == "flat":
                o_ref[:, g * LANES:(g + 1) * LANES] = zg.astype(o_ref.dtype)
            elif kind == "heads":
                o_ref[:, g, :] = zg
            elif kind == "heads_major":
                if len(o_ref.shape) == 3:
                    o_ref[g] = zg
                else:
                    gr = o_ref.shape[2]
                    for b in range(o_ref.shape[0]):
                        o_ref[b, g] = zg[b * gr:(b + 1) * gr]
            else:
                o_ref[...] = zg[:, :o_ref.shape[1]].astype(o_ref.dtype)


def _proj(h, w, layer, col0, width, outs, rope=None, name="proj"):
    n, d = h.shape
    tm = _largest_divisor(n, (1024, 512, 256, 128, 64))
    tn = _largest_divisor(width, (1024, 768, 512, 256, 128))
    assert col0 % tn == 0
    cb = col0 // tn
    shifts, tables = rope if rope else ((), ())
    in_specs = [pl.BlockSpec((tm, d), lambda j, i: (i, 0)),
                pl.BlockSpec((None, d, tn), lambda j, i: (layer, 0, cb + j))]
    operands = [h, w]
    for t in tables:
        assert t.shape == (n, LANES)
        in_specs.append(pl.BlockSpec((tm, LANES), lambda j, i: (i, 0)))
        operands.append(t)
    kinds, out_specs, out_shape, aliases = [], [], [], {}
    prevs = []
    for req in outs:
        kind = req[0]
        if kind == "flat":
            out_specs.append(pl.BlockSpec((tm, tn), lambda j, i: (i, j)))
            out_shape.append(jax.ShapeDtypeStruct((n, width), req[1]))
        elif kind == "lanes":
            assert width == LANES
            out_specs.append(pl.BlockSpec((tm, req[1]), lambda j, i: (i, 0)))
            out_shape.append(jax.ShapeDtypeStruct((n, req[1]), req[2]))
        elif kind == "heads":
            assert tn == width
            nh = width // LANES
            out_specs.append(pl.BlockSpec((None, tm, nh, LANES), lambda j, i: (layer, i, 0, 0)))
            out_shape.append(jax.ShapeDtypeStruct((req[2], n, nh, LANES), _F32))
            prevs.append((len(kinds), req[1]))
        elif kind == "heads_major":
            assert tn == width
            nh = width // LANES
            gr = req[3]
            if tm <= gr:
                per = gr // tm
                out_specs.append(pl.BlockSpec((None, None, nh, tm, LANES),
                                              lambda j, i, per=per: (layer, i // per, 0, i % per, 0)))
            else:
                out_specs.append(pl.BlockSpec((None, tm // gr, nh, gr, LANES), lambda j, i: (layer, i, 0, 0, 0)))
            out_shape.append(jax.ShapeDtypeStruct((req[2], n // gr, nh, gr, LANES), _F32))
            prevs.append((len(kinds), req[1]))
        elif kind == "lanes_stacked":
            assert width == LANES
            out_specs.append(pl.BlockSpec((None, tm, req[1]), lambda j, i: (layer, i, 0)))
            out_shape.append(jax.ShapeDtypeStruct((req[3], n, req[1]), _F32))
            prevs.append((len(kinds), req[2]))
        else:
            raise ValueError(kind)
        kinds.append(kind)
    n_alias = 0
    for out_idx, prev in prevs:
        if prev is not None:
            aliases[len(operands)] = out_idx
            in_specs.append(pl.BlockSpec(memory_space=pl.ANY))
            operands.append(prev)
            n_alias += 1
    return pl.pallas_call(
        functools.partial(_proj_kernel, shifts=tuple(shifts), kinds=tuple(kinds), n_alias=n_alias),
        grid=(width // tn, n // tm),
        in_specs=in_specs,
        out_specs=out_specs,
        out_shape=out_shape,
        input_output_aliases=aliases,
        compiler_params=_params("parallel", "parallel"),
        name=name,
    )(*operands)


def _cumsum_matrix(t):
    r = np.arange(2 * t) % t
    c = np.arange(2 * t)
    m = np.where(c[None, :] < t, r[:, None] > c[None, :], True)
    return jnp.asarray(m, dtype=_BF16)


def _sb_block(q, kj, vj, u, carry, acc, before):
    t = kj.shape[0]
    s = lax.dot_general(q, kj, _NT, preferred_element_type=_F32) * ATTN_SCALE
    sp = _softplus(s)
    lk = -sp
    if before is not None:
        lk = jnp.where(before, lk, 0.0)
    hi, lo = _split_bf16(lk)
    cr = jnp.dot(jnp.concatenate([hi, lo], axis=1), u, preferred_element_type=_F32)
    a = jnp.exp(s - sp + carry + cr[:, :t])
    if before is not None:
        a = jnp.where(before, a, 0.0)
    acc = acc + jnp.dot(a.astype(_BF16), vj, preferred_element_type=_F32)
    return carry + cr[:, t:], acc


def _sb_prompt_kernel(q_ref, k_ref, v_ref, z_ref, u_ref, o_ref, carry_ref, acc_ref, *, tq):
    i = pl.program_id(1)
    heads = SB_HEADS
    hsl = [slice(h * HEAD_DIM, (h + 1) * HEAD_DIM) for h in range(heads)]
    rsl = [slice(h * tq, (h + 1) * tq) for h in range(heads)]

    def key_tile(off, before):
        s = jnp.concatenate(
            [lax.dot_general(q_ref[:, hsl[h]], k_ref[pl.ds(off, tq), hsl[h]], _NT, preferred_element_type=_F32)
             for h in range(heads)], axis=0) * ATTN_SCALE
        sp = _softplus(s)
        lk = -sp
        if before is not None:
            lk = jnp.where(before, lk, 0.0)
        hi, lo = _split_bf16(lk)
        cr = jnp.dot(jnp.concatenate([hi, lo], axis=1), u_ref[...], preferred_element_type=_F32)
        a = jnp.exp(s - sp + carry_ref[...] + cr[:, :tq])
        if before is not None:
            a = jnp.where(before, a, 0.0)
        carry_ref[...] += cr[:, tq:]
        ab = a.astype(_BF16)
        for h in range(heads):
            acc_ref[rsl[h], :] += jnp.dot(ab[rsl[h]], v_ref[pl.ds(off, tq), hsl[h]], preferred_element_type=_F32)

    carry_ref[...] = jnp.zeros_like(carry_ref)
    acc_ref[...] = jnp.zeros_like(acc_ref)
    row = lax.broadcasted_iota(_I32, (heads * tq, tq), 0) & (tq - 1)
    col = lax.broadcasted_iota(_I32, (heads * tq, tq), 1)
    key_tile(pl.multiple_of(i * tq, tq), col < row)

    def body(t, _):
        key_tile(pl.multiple_of((i - 1 - t) * tq, tq), None)
        return 0

    lax.fori_loop(0, i, body, 0)
    for h in range(heads):
        o_ref[:, hsl[h]] = (acc_ref[rsl[h], :] * _silu(z_ref[:, hsl[h]])).astype(o_ref.dtype)


def _sb_prompt(q, k, v, z, b, t):
    tq = Q_BLOCK
    nq = t // tq
    w = SB_WIDTH
    u = _cumsum_matrix(tq)
    return pl.pallas_call(
        functools.partial(_sb_prompt_kernel, tq=tq),
        grid=(b, nq),
        in_specs=[pl.BlockSpec((tq, w), lambda bi, i: (bi * nq + i, 0)),
                  pl.BlockSpec((t, w), lambda bi, i: (bi, 0)),
                  pl.BlockSpec((t, w), lambda bi, i: (bi, 0)),
                  pl.BlockSpec((tq, w), lambda bi, i: (bi * nq + i, 0)),
                  pl.BlockSpec((2 * tq, 2 * tq), lambda bi, i: (0, 0))],
        out_specs=pl.BlockSpec((tq, w), lambda bi, i: (bi * nq + i, 0)),
        out_shape=jax.ShapeDtypeStruct(q.shape, _BF16),
        scratch_shapes=[pltpu.VMEM((SB_HEADS * tq, tq), _F32), pltpu.VMEM((SB_HEADS * tq, HEAD_DIM), _F32)],
        compiler_params=_params("parallel", "arbitrary"),
        name="sb_prompt",
    )(q, k, v, z, u)


def _select_to_bias(key_ref, bias_ref, nch, ch, kk):
    rows = key_ref.shape[0]

    def count(pred):
        def body(c, part):
            off = pl.multiple_of(c * ch, ch)
            m = jnp.where(pred(key_ref[:, pl.ds(off, ch)], off), 1, 0).astype(_I32)
            for g in range(ch // LANES):
                part = part + m[:, g * LANES:(g + 1) * LANES]
            return part
        part = lax.fori_loop(0, nch, body, jnp.zeros((rows, LANES), _I32))
        return jnp.sum(part, axis=1, keepdims=True)

    def bit_body(it, u):
        cand_u = u | lax.shift_left(jnp.int32(1), 31 - it)
        cand = cand_u ^ jnp.int32(INT32_MIN)
        cnt = count(lambda kc, off: kc >= cand)
        return jnp.where(cnt >= kk, cand_u, u)

    u = lax.fori_loop(0, 32, bit_body, jnp.zeros((rows, 1), _I32))
    thr = u ^ jnp.int32(INT32_MIN)
    cnt_gt = count(lambda kc, off: kc > thr)
    cnt_ge = count(lambda kc, off: kc >= thr)
    need = kk - cnt_gt
    excess = jnp.where((cnt_ge > kk) & (thr > KEY_NEG_INF), 1, 0).astype(_I32)
    nbits = max(1, int(math.ceil(math.log2(key_ref.shape[1] + 1))))

    def tie_cut(_):
        def q_body(it, q):
            cand = q | lax.shift_left(jnp.int32(1), nbits - 1 - it)

            def pred(kc, off):
                idx = off + lax.broadcasted_iota(_I32, kc.shape, 1)
                return (kc == thr) & (idx < cand)
            return jnp.where(count(pred) < need, cand, q)
        return lax.fori_loop(0, nbits, q_body, jnp.zeros((rows, 1), _I32))

    def no_cut(_):
        return jnp.full((rows, 1), 2 ** 30, _I32)

    qcut = lax.cond(jnp.max(excess) > 0, tie_cut, no_cut, 0)

    def write(c, _):
        off = pl.multiple_of(c * ch, ch)
        kc = key_ref[:, pl.ds(off, ch)]
        idx = off + lax.broadcasted_iota(_I32, kc.shape, 1)
        sel = (kc > thr) | ((kc == thr) & (idx <= qcut))
        ok = sel & (kc > KEY_NEG_INF)
        bias_ref[:, pl.ds(off, ch)] = jnp.where(ok, 0.0, NEG_BIAS).astype(_F32)
        return 0

    lax.fori_loop(0, nch, write, 0)


def _idx_prompt_kernel(qi_ref, wi_ref, ki_ref, bias_ref, key_ref, *, tq, ch, kk):
    i = pl.program_id(1)
    nch = (i * tq + tq + ch - 1) // ch
    qi = qi_ref[...]
    wi = wi_ref[...]
    rowpos = i * tq + lax.broadcasted_iota(_I32, (tq, ch), 0)

    def score(c, _):
        off = pl.multiple_of(c * ch, ch)
        kc = ki_ref[pl.ds(off, ch), :]
        acc = jnp.zeros((tq, ch), _F32)
        for h in range(IDX_HEADS):
            r = lax.dot_general(qi[:, h * IDX_DIM:(h + 1) * IDX_DIM], kc, _NT,
                                preferred_element_type=_F32)
            acc = acc + jnp.maximum(r, 0.0) * wi[:, IDX_DIM + h:IDX_DIM + h + 1]
        sc = acc * IDX_SCALE
        sc = jnp.where(sc == 0.0, 0.0, sc)
        colpos = off + lax.broadcasted_iota(_I32, (tq, ch), 1)
        sc = jnp.where(colpos <= rowpos, sc, -jnp.inf)
        key_ref[:, pl.ds(off, ch)] = _sortable_key(sc)
        return 0

    lax.fori_loop(0, nch, score, 0)
    bias_ref[...] = jnp.full(bias_ref.shape, NEG_BIAS, _F32)
    _select_to_bias(key_ref, bias_ref, nch, ch, kk)


def _idx_prompt(qi, kiwi, ki, b, t, kk):
    tq = Q_BLOCK
    nq = t // tq
    ch = _largest_divisor(t, (512, 256, 128))
    return pl.pallas_call(
        functools.partial(_idx_prompt_kernel, tq=tq, ch=ch, kk=kk),
        grid=(b, nq),
        in_specs=[pl.BlockSpec((tq, IDX_WIDTH), lambda bi, i: (bi * nq + i, 0)),
                  pl.BlockSpec((tq, LANES), lambda bi, i: (bi * nq + i, 0)),
                  pl.BlockSpec((t, IDX_DIM), lambda bi, i: (bi, 0))],
        out_specs=pl.BlockSpec((tq, t), lambda bi, i: (bi * nq + i, 0)),
        out_shape=jax.ShapeDtypeStruct((b * t, t), _F32),
        scratch_shapes=[pltpu.VMEM((tq, t), _I32)],
        compiler_params=_params("parallel", "arbitrary"),
        name="idx_select_prompt",
    )(qi, kiwi, ki)


def _dsa_prompt_kernel(q_ref, k_ref, v_ref, bias_ref, z_ref, o_ref, m_ref, l_ref, acc_ref, *, tq, ch):
    i = pl.program_id(1)
    nch = (i * tq + tq + ch - 1) // ch
    heads = DSA_HEADS
    hsl = [slice(h * HEAD_DIM, (h + 1) * HEAD_DIM) for h in range(heads)]
    rsl = [slice(h * tq, (h + 1) * tq) for h in range(heads)]
    m_ref[...] = jnp.full(m_ref.shape, NEG_BIAS, _F32)
    l_ref[...] = jnp.zeros_like(l_ref)
    acc_ref[...] = jnp.zeros_like(acc_ref)
    ones = jnp.ones((ch, LANES), _BF16)

    def body(c, _):
        off = pl.multiple_of(c * ch, ch)
        bias = bias_ref[:, pl.ds(off, ch)]
        s = jnp.concatenate(
            [lax.dot_general(q_ref[:, hsl[h]], k_ref[pl.ds(off, ch), hsl[h]], _NT, preferred_element_type=_F32)
             for h in range(heads)], axis=0) * ATTN_SCALE + jnp.concatenate([bias] * heads, axis=0)
        m_old = m_ref[...]
        m_new = jnp.maximum(m_old, jnp.max(s, axis=1, keepdims=True))
        alpha = jnp.exp(m_old - m_new)
        pb = jnp.exp(s - m_new).astype(_BF16)
        l_ref[...] = l_ref[...] * alpha + jnp.dot(pb, ones, preferred_element_type=_F32)
        for h in range(heads):
            acc_ref[rsl[h], :] = acc_ref[rsl[h], :] * alpha[rsl[h]] + jnp.dot(
                pb[rsl[h]], v_ref[pl.ds(off, ch), hsl[h]], preferred_element_type=_F32)
        m_ref[...] = m_new
        return 0

    lax.fori_loop(0, nch, body, 0)
    for h in range(heads):
        o_ref[:, hsl[h]] = (acc_ref[rsl[h], :] / l_ref[rsl[h], :] * _silu(z_ref[:, hsl[h]])).astype(o_ref.dtype)


def _dsa_prompt(q, k, v, bias, z, b, t):
    tq = Q_BLOCK
    nq = t // tq
    ch = _largest_divisor(t, (512, 256, 128))
    w = DSA_WIDTH
    return pl.pallas_call(
        functools.partial(_dsa_prompt_kernel, tq=tq, ch=ch),
        grid=(b, nq),
        in_specs=[pl.BlockSpec((tq, w), lambda bi, i: (bi * nq + i, 0)),
                  pl.BlockSpec((t, w), lambda bi, i: (bi, 0)),
                  pl.BlockSpec((t, w), lambda bi, i: (bi, 0)),
                  pl.BlockSpec((tq, t), lambda bi, i: (bi * nq + i, 0)),
                  pl.BlockSpec((tq, w), lambda bi, i: (bi * nq + i, 0))],
        out_specs=pl.BlockSpec((tq, w), lambda bi, i: (bi * nq + i, 0)),
        out_shape=jax.ShapeDtypeStruct(q.shape, _BF16),
        scratch_shapes=[pltpu.VMEM((DSA_HEADS * tq, 1), _F32), pltpu.VMEM((DSA_HEADS * tq, LANES), _F32),
                        pltpu.VMEM((DSA_HEADS * tq, HEAD_DIM), _F32)],
        compiler_params=_params("parallel", "arbitrary"),
        name="dsa_prompt",
    )(q, k, v, bias, z)


def _mem_kernel(q_ref, mk_ref, mv_ref, z_ref, o_ref):
    tm = q_ref.shape[0]
    pad = (-tm) % 16
    by_head = len(mk_ref.shape) == 3
    for h in range(MEM_HEADS):
        hs = slice(h * HEAD_DIM, (h + 1) * HEAD_DIM)
        mk = mk_ref[:, h, :] if by_head else mk_ref[:, hs]
        mv = mv_ref[:, h, :] if by_head else mv_ref[:, hs]
        qh = q_ref[:, hs].astype(_F32)
        if pad:
            qh = jnp.concatenate([qh, jnp.zeros((pad, HEAD_DIM), _F32)], axis=0)
        s = lax.dot_general(qh.astype(_BF16), mk.astype(_BF16), _NT, preferred_element_type=_F32) * ATTN_SCALE
        p = jnp.exp(s - jnp.max(s, axis=1, keepdims=True))
        p = p / jnp.sum(p, axis=1, keepdims=True)
        o = jnp.dot(p.astype(_BF16), mv.astype(_BF16), preferred_element_type=_F32)
        o_ref[:, hs] = (o[:tm] * _silu(z_ref[:, hs])).astype(o_ref.dtype)


def _mem_attend(q, mk, mv, z, layer=None):
    g, r, w = q.shape
    tm = _largest_divisor(r, (512, 256, 128, 64, 8))
    if layer is None:
        m = mk.shape[1]
        mem = pl.BlockSpec((None, m, w), lambda gi, i: (gi, 0, 0))
    else:
        m = mk.shape[2]
        mem = pl.BlockSpec((None, None, m, MEM_HEADS, HEAD_DIM), lambda gi, i: (layer, gi, 0, 0, 0))
    return pl.pallas_call(
        _mem_kernel,
        grid=(g, r // tm),
        in_specs=[pl.BlockSpec((None, tm, w), lambda gi, i: (gi, i, 0)),
                  mem, mem,
                  pl.BlockSpec((None, tm, w), lambda gi, i: (gi, i, 0))],
        out_specs=pl.BlockSpec((None, tm, w), lambda gi, i: (gi, i, 0)),
        out_shape=jax.ShapeDtypeStruct((g, r, w), _BF16),
        compiler_params=_params("parallel", "parallel"),
        name="mem_attend",
    )(q, mk, mv, z)


def _mix_kernel(gs_ref, gd_ref, gm_ref, g1_ref, g2_ref, g3_ref, w1_ref, w2_ref, w3_ref, o_ref):
    a = jnp.dot(gs_ref[...], w1_ref[...], preferred_element_type=_F32)
    c = jnp.dot(gd_ref[...], w2_ref[...], preferred_element_type=_F32)
    m = jnp.dot(gm_ref[...], w3_ref[...], preferred_element_type=_F32)
    mixed = _sigmoid(g1_ref[...]) * a + _sigmoid(g2_ref[...]) * c + _sigmoid(g3_ref[...]) * m
    o_ref[...] = mixed.astype(o_ref.dtype)


def _mix(gs, gd, gm, gates, w1, w2, w3, layer):
    n = gs.shape[0]
    d = w1.shape[2]
    tm = _largest_divisor(n, (256, 128, 64))
    row = lambda w: pl.BlockSpec((tm, w), lambda i: (i, 0))
    wspec = lambda w: pl.BlockSpec((None, w.shape[1], d), lambda i: (layer, 0, 0))
    return pl.pallas_call(
        _mix_kernel,
        grid=(n // tm,),
        in_specs=[row(SB_WIDTH), row(DSA_WIDTH), row(MEM_WIDTH),
                  pl.BlockSpec((tm, d), lambda i: (i, 0)),
                  pl.BlockSpec((tm, d), lambda i: (i, 1)),
                  pl.BlockSpec((tm, d), lambda i: (i, 2)),
                  wspec(w1), wspec(w2), wspec(w3)],
        out_specs=pl.BlockSpec((tm, d), lambda i: (i, 0)),
        out_shape=jax.ShapeDtypeStruct((n, d), _BF16),
        compiler_params=_params("parallel"),
        name="mix",
    )(gs, gd, gm, gates, gates, gates, w1, w2, w3)


def _out_kernel(x_ref, m_ref, w_ref, g_ref, xo_ref, ho_ref):
    xn = x_ref[...] + jnp.dot(m_ref[...], w_ref[...], preferred_element_type=_F32)
    xo_ref[...] = xn
    y = xn * lax.rsqrt(jnp.mean(xn * xn, axis=-1, keepdims=True) + NORM_EPS)
    ho_ref[...] = (y * g_ref[...]).astype(ho_ref.dtype)


def _out_proj(x, mixed, w, layer, g, h_dtype):
    n, d = x.shape
    tm = _largest_divisor(n, (512, 256, 128, 64))
    return pl.pallas_call(
        _out_kernel,
        grid=(n // tm,),
        in_specs=[pl.BlockSpec((tm, d), lambda i: (i, 0)),
                  pl.BlockSpec((tm, d), lambda i: (i, 0)),
                  pl.BlockSpec((None, d, d), lambda i: (layer, 0, 0)),
                  pl.BlockSpec((1, d), lambda i: (0, 0))],
        out_specs=[pl.BlockSpec((tm, d), lambda i: (i, 0)),
                   pl.BlockSpec((tm, d), lambda i: (i, 0))],
        out_shape=[jax.ShapeDtypeStruct((n, d), _F32), jax.ShapeDtypeStruct((n, d), h_dtype)],
        compiler_params=_params("parallel"),
        name="out_proj",
    )(x, mixed, w, g.reshape(1, d))


def _head_rows(x, ds, h):
    return x[h * ds:(h + 1) * ds]


def _stack_scores(qs, key_heads, ds):
    parts = []
    for h, kh in enumerate(key_heads):
        r = lax.dot_general(qs, kh, _NT, preferred_element_type=_F32)
        parts.append(_head_rows(r, ds, h))
    return jnp.concatenate(parts, axis=0) * ATTN_SCALE


def _stack_pv(p, val_heads, ds):
    pb = p.astype(_BF16)
    parts = []
    for h, vh in enumerate(val_heads):
        r = jnp.dot(pb, vh, preferred_element_type=_F32)
        parts.append(_head_rows(r, ds, h))
    return jnp.concatenate(parts, axis=0)


def _pad_rows(x, rows):
    return jnp.concatenate([x, jnp.zeros((rows - x.shape[0], x.shape[1]), x.dtype)], axis=0)


def _page_heads(ref):
    return [ref[h].astype(_BF16) for h in range(ref.shape[0])]


def _new_heads(ref, heads):
    x = _pad_rows(ref[...], PAGE_SIZE).astype(_BF16)
    return [x[:, h * HEAD_DIM:(h + 1) * HEAD_DIM] for h in range(heads)]


def _pages_per_step(n_pages, candidates):
    return _largest_divisor(n_pages, candidates)


def _sb_sample_kernel(pt_ref, q_ref, kn_ref, vn_ref, *rest, ds, n_steps, pp):
    kc, vc = rest[:pp], rest[pp:2 * pp]
    z_ref, u_ref, o_ref, carry_ref, acc_ref = rest[2 * pp:]
    p = pl.program_id(1)
    rows = SB_HEADS * ds
    qs = q_ref[...]

    def page(key_heads, val_heads, before, carry, acc):
        s = _stack_scores(qs, key_heads, ds)
        sp = _softplus(s)
        lk = -sp
        if before is not None:
            lk = jnp.where(before, lk, 0.0)
        hi, lo = _split_bf16(lk)
        cr = jnp.dot(jnp.concatenate([hi, lo], axis=1), u_ref[...], preferred_element_type=_F32)
        a = jnp.exp(s - sp + carry + cr[:, :PAGE_SIZE])
        if before is not None:
            a = jnp.where(before, a, 0.0)
        return carry + cr[:, PAGE_SIZE:], acc + _stack_pv(a, val_heads, ds)

    @pl.when(p == 0)
    def _():
        qrow = lax.broadcasted_iota(_I32, (rows, PAGE_SIZE), 0) % ds
        col = lax.broadcasted_iota(_I32, (rows, PAGE_SIZE), 1)
        carry, acc = page(_new_heads(kn_ref, SB_HEADS), _new_heads(vn_ref, SB_HEADS), col < qrow,
                          jnp.zeros((rows, PAGE_SIZE), _F32), jnp.zeros((rows, HEAD_DIM), _F32))
        carry_ref[...] = carry
        acc_ref[...] = acc

    carry, acc = carry_ref[...], acc_ref[...]
    s = jnp.concatenate([_stack_scores(qs, _page_heads(kc[k]), ds) for k in range(pp)], axis=0)
    sp = _softplus(s)
    hi, lo = _split_bf16(-sp)
    cr = jnp.dot(jnp.concatenate([hi, lo], axis=1), u_ref[...], preferred_element_type=_F32)
    weights = []
    for k in range(pp):
        r = slice(k * rows, (k + 1) * rows)
        weights.append(jnp.exp(s[r] - sp[r] + carry + cr[r, :PAGE_SIZE]))
        carry = carry + cr[r, PAGE_SIZE:]
    for k in range(pp):
        acc = acc + _stack_pv(weights[k], _page_heads(vc[k]), ds)
    carry_ref[...] = carry
    acc_ref[...] = acc

    @pl.when(p == n_steps - 1)
    def _():
        o_ref[...] = (acc * _silu(z_ref[...])).astype(o_ref.dtype)


def _sb_sample(qs, kn, vn, cache_k, cache_v, zs, page_table, layer):
    db, rows, _ = qs.shape
    ds = rows // SB_HEADS
    n_pages = page_table.shape[1]
    pp = _pages_per_step(n_pages, (8, 4, 2, 1))
    n_steps = n_pages // pp
    w = SB_WIDTH
    u = _cumsum_matrix(PAGE_SIZE)
    seq = lambda r, c: pl.BlockSpec((None, r, c), lambda b, p, pt: (b, 0, 0))
    pages = [pl.BlockSpec((None, None, SB_HEADS, PAGE_SIZE, HEAD_DIM),
                          lambda b, p, pt, k=k: (layer, pt[b, n_pages - 1 - (p * pp + k)], 0, 0, 0))
             for k in range(pp)]
    return pl.pallas_call(
        functools.partial(_sb_sample_kernel, ds=ds, n_steps=n_steps, pp=pp),
        grid_spec=pltpu.PrefetchScalarGridSpec(
            num_scalar_prefetch=1,
            grid=(db, n_steps),
            in_specs=[seq(rows, HEAD_DIM), seq(ds, w), seq(ds, w)] + pages + pages
                     + [seq(rows, HEAD_DIM), pl.BlockSpec((2 * PAGE_SIZE, 2 * PAGE_SIZE), lambda b, p, pt: (0, 0))],
            out_specs=seq(rows, HEAD_DIM),
            scratch_shapes=[pltpu.VMEM((rows, PAGE_SIZE), _F32), pltpu.VMEM((rows, HEAD_DIM), _F32)]),
        out_shape=jax.ShapeDtypeStruct(qs.shape, _BF16),
        compiler_params=_params("parallel", "arbitrary"),
        name="sb_sample",
    )(page_table, qs, kn, vn, *([cache_k] * pp), *([cache_v] * pp), zs, u)


def _idx_sample_kernel(pt_ref, qi_ref, wi_ref, kn_ref, *rest, ds, pp):
    kc = rest[:pp]
    oc_ref, on_ref = rest[pp:]
    p = pl.program_id(1)

    def scores(r):
        r = jnp.maximum(r, 0.0) * wi_ref[...]
        sc = _head_rows(r, ds, 0)
        for h in range(1, IDX_HEADS):
            sc = sc + _head_rows(r, ds, h)
        sc = sc * IDX_SCALE
        return jnp.where(sc == 0.0, 0.0, sc)

    for k in range(pp):
        r = jnp.dot(qi_ref[...], kc[k][...].astype(_BF16), preferred_element_type=_F32)
        oc_ref[:, k * PAGE_SIZE:(k + 1) * PAGE_SIZE] = _sortable_key(scores(r))

    @pl.when(p == 0)
    def _():
        kn = _pad_rows(kn_ref[...], PAGE_SIZE).astype(_BF16)
        sc = scores(lax.dot_general(qi_ref[...], kn, _NT, preferred_element_type=_F32))
        s_row = lax.broadcasted_iota(_I32, (ds, PAGE_SIZE), 0)
        col = lax.broadcasted_iota(_I32, (ds, PAGE_SIZE), 1)
        on_ref[...] = _sortable_key(jnp.where(col <= s_row, sc, -jnp.inf))


def _idx_sample(qi_t, wi_col, ki_new, cache_ik, page_table, layer):
    db, rows, _ = qi_t.shape
    ds = rows // IDX_HEADS
    n_pages = page_table.shape[1]
    pp = _pages_per_step(n_pages, (16, 8, 4, 2, 1))
    seq = lambda r, c: pl.BlockSpec((None, r, c), lambda b, p, pt: (b, 0, 0))
    pages = [pl.BlockSpec((None, None, IDX_DIM, PAGE_SIZE),
                          lambda b, p, pt, k=k: (layer, pt[b, p * pp + k], 0, 0)) for k in range(pp)]
    return pl.pallas_call(
        functools.partial(_idx_sample_kernel, ds=ds, pp=pp),
        grid_spec=pltpu.PrefetchScalarGridSpec(
            num_scalar_prefetch=1,
            grid=(db, n_pages // pp),
            in_specs=[seq(rows, IDX_DIM), seq(rows, 1), seq(ds, IDX_DIM)] + pages,
            out_specs=[pl.BlockSpec((None, ds, pp * PAGE_SIZE), lambda b, p, pt: (b, 0, p)),
                       seq(ds, PAGE_SIZE)]),
        out_shape=[jax.ShapeDtypeStruct((db, ds, n_pages * PAGE_SIZE), _I32),
                   jax.ShapeDtypeStruct((db, ds, PAGE_SIZE), _I32)],
        compiler_params=_params("parallel", "arbitrary"),
        name="idx_scores_sample",
    )(page_table, qi_t, wi_col, ki_new, *([cache_ik] * pp))


def _select_kernel(kc_ref, kn_ref, bc_ref, bn_ref, key_ref, bias_ref, *, ch, kk):
    n_past = kc_ref.shape[1]
    key_ref[:, :n_past] = kc_ref[...]
    key_ref[:, n_past:] = kn_ref[...]
    _select_to_bias(key_ref, bias_ref, key_ref.shape[1] // ch, ch, kk)
    bc_ref[...] = bias_ref[:, :n_past]
    bn_ref[...] = bias_ref[:, n_past:]


def _select_sample(keys_cache, keys_new, kk):
    rows, n_past = keys_cache.shape
    s = n_past + keys_new.shape[1]
    ch = _largest_divisor(s, (512, 384, 256, 128))
    full = lambda c: pl.BlockSpec((rows, c), lambda i: (0, 0))
    return pl.pallas_call(
        functools.partial(_select_kernel, ch=ch, kk=kk),
        grid=(1,),
        in_specs=[full(n_past), full(PAGE_SIZE)],
        out_specs=[full(n_past), full(PAGE_SIZE)],
        out_shape=[jax.ShapeDtypeStruct((rows, n_past), _F32), jax.ShapeDtypeStruct((rows, PAGE_SIZE), _F32)],
        scratch_shapes=[pltpu.VMEM((rows, s), _I32), pltpu.VMEM((rows, s), _F32)],
        compiler_params=_params("arbitrary"),
        name="select_sample",
    )(keys_cache, keys_new)


def _dsa_sample_kernel(pt_ref, q_ref, kn_ref, vn_ref, *rest, ds, n_steps, pp):
    kc, vc = rest[:pp], rest[pp:2 * pp]
    bn_ref, bc_ref, z_ref, o_ref, m_ref, l_ref, acc_ref = rest[2 * pp:]
    p = pl.program_id(1)
    qs = q_ref[...]

    def step(key_pages, val_pages, bias, m_old, l_old, acc_old):
        s = jnp.concatenate([_stack_scores(qs, kh, ds) for kh in key_pages], axis=1)
        s = s + jnp.concatenate([bias] * DSA_HEADS, axis=0)
        m_new = jnp.maximum(m_old, jnp.max(s, axis=1, keepdims=True))
        alpha = jnp.exp(m_old - m_new)
        pr = jnp.exp(s - m_new)
        pv = _stack_pv(pr[:, :PAGE_SIZE], val_pages[0], ds)
        for k in range(1, len(val_pages)):
            pv = pv + _stack_pv(pr[:, k * PAGE_SIZE:(k + 1) * PAGE_SIZE], val_pages[k], ds)
        return m_new, l_old * alpha + jnp.sum(pr, axis=1, keepdims=True), acc_old * alpha + pv

    @pl.when(p == 0)
    def _():
        rows = DSA_HEADS * ds
        m, l, acc = step([_new_heads(kn_ref, DSA_HEADS)], [_new_heads(vn_ref, DSA_HEADS)], bn_ref[...],
                         jnp.full((rows, 1), NEG_BIAS, _F32), jnp.zeros((rows, 1), _F32),
                         jnp.zeros((rows, HEAD_DIM), _F32))
        m_ref[...] = m
        l_ref[...] = l
        acc_ref[...] = acc

    m, l, acc = step([_page_heads(r) for r in kc], [_page_heads(r) for r in vc], bc_ref[...],
                     m_ref[...], l_ref[...], acc_ref[...])
    m_ref[...] = m
    l_ref[...] = l
    acc_ref[...] = acc

    @pl.when(p == n_steps - 1)
    def _():
        o_ref[...] = (acc / l * _silu(z_ref[...])).astype(o_ref.dtype)


def _dsa_sample(qs, kn, vn, cache_k, cache_v, bias_cache, bias_new, zs, page_table, layer):
    db, rows, _ = qs.shape
    ds = rows // DSA_HEADS
    n_pages = page_table.shape[1]
    pp = _pages_per_step(n_pages, (8, 4, 2, 1))
    n_steps = n_pages // pp
    w = DSA_WIDTH
    seq = lambda r, c: pl.BlockSpec((None, r, c), lambda b, p, pt: (b, 0, 0))
    pages = [pl.BlockSpec((None, None, DSA_HEADS, PAGE_SIZE, HEAD_DIM),
                          lambda b, p, pt, k=k: (layer, pt[b, p * pp + k], 0, 0, 0)) for k in range(pp)]
    return pl.pallas_call(
        functools.partial(_dsa_sample_kernel, ds=ds, n_steps=n_steps, pp=pp),
        grid_spec=pltpu.PrefetchScalarGridSpec(
            num_scalar_prefetch=1,
            grid=(db, n_steps),
            in_specs=[seq(rows, HEAD_DIM), seq(ds, w), seq(ds, w)] + pages + pages
                     + [seq(ds, PAGE_SIZE),
                        pl.BlockSpec((None, ds, pp * PAGE_SIZE), lambda b, p, pt: (b, 0, p)),
                        seq(rows, HEAD_DIM)],
            out_specs=seq(rows, HEAD_DIM),
            scratch_shapes=[pltpu.VMEM((rows, 1), _F32), pltpu.VMEM((rows, 1), _F32),
                            pltpu.VMEM((rows, HEAD_DIM), _F32)]),
        out_shape=jax.ShapeDtypeStruct(qs.shape, _BF16),
        compiler_params=_params("parallel", "arbitrary"),
        name="dsa_sample",
    )(page_table, qs, kn, vn, *([cache_k] * pp), *([cache_v] * pp), bias_new, bias_cache, zs)


def _rope_tables(pos, reps):
    def base(half):
        inv_freq = jnp.power(jnp.float32(ROPE_THETA), -jnp.arange(half, dtype=_F32) / half)
        ang = pos.astype(_F32)[:, None] * inv_freq[None, :]
        return jnp.cos(ang), jnp.sin(ang)

    tile = lambda a: jnp.tile(a, (reps, 1))
    c, s = base(HEAD_DIM // 2)
    head = (tile(jnp.concatenate([c, c], axis=1)), tile(jnp.concatenate([-s, s], axis=1)))
    c, s = base(IDX_DIM // 2)
    z = jnp.zeros_like(s)
    c64, lo64, hi64 = (jnp.concatenate([c, c], axis=1), jnp.concatenate([-s, z], axis=1),
                       jnp.concatenate([z, s], axis=1))
    idx = tuple(tile(jnp.concatenate([a, a], axis=1)) for a in (c64, lo64, hi64))
    one, zero = jnp.ones_like(c64), jnp.zeros_like(c64)
    kiwi = (tile(jnp.concatenate([c64, one], axis=1)), tile(jnp.concatenate([lo64, zero], axis=1)),
            tile(jnp.concatenate([hi64, zero], axis=1)))
    return head, idx, kiwi


_ROPE128 = (LANES // 2,)
_ROPE64 = (LANES - IDX_DIM // 2, IDX_DIM // 2)

_OFF = dict(q_sb=0, k_sb=768, v_sb=1536, z_sb=2304, q_d=3072, k_d=3840, v_d=4608, z_d=5376, qi=6144)
_KIWI0 = 7168
_TAIL0 = _KIWI0 + IDX_DIM + IDX_HEADS


def _project_all(h, w_main, w_kiwi, w_tail, layer, tabs, d_model, depth, st, kv_dtype, seq_len):
    head_t, idx_t, kiwi_t = tabs
    pj = functools.partial(_proj, h)
    kv = lambda name: [("heads_major", st[name], depth, seq_len), ("flat", kv_dtype)]
    o = {}
    (o["q_sb"],) = pj(w_main, layer, _OFF["q_sb"], SB_WIDTH, [("flat", _BF16)], name="proj_q_sb")
    st["k_sb"], o["k_sb"] = pj(w_main, layer, _OFF["k_sb"], SB_WIDTH, kv("k_sb"), name="proj_k_sb")
    st["v_sb"], o["v_sb"] = pj(w_main, layer, _OFF["v_sb"], SB_WIDTH, kv("v_sb"), name="proj_v_sb")
    (o["z_sb"],) = pj(w_main, layer, _OFF["z_sb"], SB_WIDTH, [("flat", _F32)], name="proj_z_sb")
    (o["q_d"],) = pj(w_main, layer, _OFF["q_d"], DSA_WIDTH, [("flat", _BF16)], rope=(_ROPE128, head_t),
                     name="proj_q_d")
    st["k_d"], o["k_d"] = pj(w_main, layer, _OFF["k_d"], DSA_WIDTH, kv("k_d"), rope=(_ROPE128, head_t),
                             name="proj_k_d")
    st["v_d"], o["v_d"] = pj(w_main, layer, _OFF["v_d"], DSA_WIDTH, kv("v_d"), name="proj_v_d")
    (o["z_d"],) = pj(w_main, layer, _OFF["z_d"], DSA_WIDTH, [("flat", _F32)], name="proj_z_d")
    (o["qi"],) = pj(w_main, layer, _OFF["qi"], IDX_WIDTH, [("flat", _BF16)], rope=(_ROPE64, idx_t), name="proj_qi")
    o["kiwi"], o["ki_b"], st["ki"] = pj(
        w_kiwi, layer, 0, LANES,
        [("flat", _F32), ("lanes", IDX_DIM, _BF16), ("lanes_stacked", IDX_DIM, st["ki"], depth)],
        rope=(_ROPE64, kiwi_t), name="proj_kiwi")
    (o["q_m"],) = pj(w_tail, layer, 0, MEM_WIDTH, [("flat", _F32)], name="proj_q_m")
    (o["z_m"],) = pj(w_tail, layer, MEM_WIDTH, MEM_WIDTH, [("flat", _F32)], name="proj_z_m")
    (o["gates"],) = pj(w_tail, layer, 2 * MEM_WIDTH, 3 * d_model, [("flat", _F32)], name="proj_gates")
    return o


def kernel(x_prompt, x_sample, cache_sb_k, cache_sb_v, cache_dsa_k, cache_dsa_v, cache_idx_k, cache_mem_k,
           cache_mem_v, page_table, mem_prompt, norm_g, w_in, w_up_sb, w_up_dsa, w_up_mem, w_out, w_mem_k,
           w_mem_v, final_norm_g):
    b, t, d = x_prompt.shape
    db, ds, _ = x_sample.shape
    depth = w_in.shape[0]
    n_pool = cache_sb_k.shape[1]
    n_pages = page_table.shape[1]
    n_past = n_pages * PAGE_SIZE
    mem_len = mem_prompt.shape[1]
    topk_p = min(TOPK_MAX, t // 4)
    topk_s = min(TOPK_MAX, (n_past + ds) // 4)
    assert t % Q_BLOCK == 0 and w_in.shape[2] == _TAIL0 + 2 * MEM_WIDTH + 3 * d

    w_main = w_in[:, :, :_KIWI0].astype(_BF16)
    w_kiwi = jnp.pad(w_in[:, :, _KIWI0:_TAIL0], ((0, 0), (0, 0), (0, LANES - (_TAIL0 - _KIWI0)))).astype(_BF16)
    w_tail = w_in[:, :, _TAIL0:].astype(_BF16)
    w_up_sb_b, w_up_dsa_b, w_up_mem_b = (w.astype(_BF16) for w in (w_up_sb, w_up_dsa, w_up_mem))
    w_out_b = w_out.astype(_BF16)
    w_mem_k_b, w_mem_v_b = w_mem_k.astype(_BF16), w_mem_v.astype(_BF16)
    gains = jnp.concatenate([norm_g, final_norm_g[None]], axis=0)

    tabs_p = _rope_tables(jnp.arange(t), b)
    tabs_s = _rope_tables(n_past + jnp.arange(ds), db)

    n = b * t
    x = x_prompt.reshape(n, d)
    h = _rmsnorm(x, gains[0], _BF16)
    mem_b = mem_prompt.reshape(b * mem_len, d).astype(_BF16)
    st_p = dict.fromkeys(("k_sb", "v_sb", "k_d", "v_d", "ki", "mk", "mv"))
    for l in range(depth):
        o = _project_all(h, w_main, w_kiwi, w_tail, l, tabs_p, d, depth, st_p, _BF16, t)
        g_sb = _sb_prompt(o["q_sb"], o["k_sb"], o["v_sb"], o["z_sb"], b, t)
        bias = _idx_prompt(o["qi"], o["kiwi"], o["ki_b"], b, t, topk_p)
        g_d = _dsa_prompt(o["q_d"], o["k_d"], o["v_d"], bias, o["z_d"], b, t)
        st_p["mk"], mk = _proj(mem_b, w_mem_k_b, l, 0, MEM_WIDTH, [("heads", st_p["mk"], depth), ("flat", _BF16)],
                               name="proj_mem_k")
        st_p["mv"], mv = _proj(mem_b, w_mem_v_b, l, 0, MEM_WIDTH, [("heads", st_p["mv"], depth), ("flat", _BF16)],
                               name="proj_mem_v")
        g_m = _mem_attend(o["q_m"].reshape(b, t, MEM_WIDTH), mk.reshape(b, mem_len, MEM_WIDTH),
                          mv.reshape(b, mem_len, MEM_WIDTH), o["z_m"].reshape(b, t, MEM_WIDTH))
        mixed = _mix(g_sb, g_d, g_m.reshape(n, MEM_WIDTH), o["gates"], w_up_sb_b, w_up_dsa_b, w_up_mem_b, l)
        last = l == depth - 1
        x, h = _out_proj(x, mixed, w_out_b, l, gains[l + 1], _F32 if last else _BF16)
    y_prompt = h.reshape(b, t, d)

    ns = db * ds
    x = x_sample.reshape(ns, d)
    h = _rmsnorm(x, gains[0], _BF16)

    def stack(a, heads):
        c = a.shape[1] // heads
        return a.reshape(db, ds, heads, c).transpose(0, 2, 1, 3).reshape(db, heads * ds, c)

    def unstack(a, heads):
        c = a.shape[2]
        return a.reshape(db, heads, ds, c).transpose(0, 2, 1, 3).reshape(db * ds, heads * c)

    by_head = lambda c: jnp.transpose(c, (0, 1, 3, 2, 4))
    c_sb_k, c_sb_v, c_d_k, c_d_v = (by_head(c) for c in (cache_sb_k, cache_sb_v, cache_dsa_k, cache_dsa_v))
    c_idx = jnp.transpose(cache_idx_k, (0, 1, 3, 2))

    st_s = dict.fromkeys(("k_sb", "v_sb", "k_d", "v_d", "ki"))
    for l in range(depth):
        o = _project_all(h, w_main, w_kiwi, w_tail, l, tabs_s, d, depth, st_s, _F32, ds)
        ki = o["kiwi"][:, :IDX_DIM]
        wi = o["kiwi"][:, IDX_DIM:IDX_DIM + IDX_HEADS]
        g_sb = _sb_sample(stack(o["q_sb"], SB_HEADS), o["k_sb"].reshape(db, ds, SB_WIDTH),
                          o["v_sb"].reshape(db, ds, SB_WIDTH), c_sb_k, c_sb_v,
                          stack(o["z_sb"], SB_HEADS), page_table, l)
        keys_c, keys_n = _idx_sample(stack(o["qi"], IDX_HEADS), stack(wi, IDX_HEADS),
                                     ki.reshape(db, ds, IDX_DIM), c_idx, page_table, l)
        bias_c, bias_n = _select_sample(keys_c.reshape(ns, n_past), keys_n.reshape(ns, PAGE_SIZE), topk_s)
        g_d = _dsa_sample(stack(o["q_d"], DSA_HEADS), o["k_d"].reshape(db, ds, DSA_WIDTH),
                          o["v_d"].reshape(db, ds, DSA_WIDTH), c_d_k, c_d_v,
                          bias_c.reshape(db, ds, n_past), bias_n.reshape(db, ds, PAGE_SIZE),
                          stack(o["z_d"], DSA_HEADS), page_table, l)
        g_m = _mem_attend(o["q_m"].reshape(db, ds, MEM_WIDTH), cache_mem_k, cache_mem_v,
                          o["z_m"].reshape(db, ds, MEM_WIDTH), layer=l)
        mixed = _mix(unstack(g_sb, SB_HEADS), unstack(g_d, DSA_HEADS), g_m.reshape(ns, MEM_WIDTH), o["gates"],
                     w_up_sb_b, w_up_dsa_b, w_up_mem_b, l)
        last = l == depth - 1
        x, h = _out_proj(x, mixed, w_out_b, l, gains[l + 1], _F32 if last else _BF16)
    y_sample = h.reshape(db, ds, d)

    kv = lambda a: jnp.transpose(a, (0, 1, 3, 2, 4))
    hp = lambda a: a.reshape((depth, b, -1) + a.shape[2:])
    hs = lambda a: a.reshape((depth, db, ds) + a.shape[2:])
    return (y_prompt, y_sample,
            kv(st_p["k_sb"]), kv(st_p["v_sb"]), kv(st_p["k_d"]), kv(st_p["v_d"]), hp(st_p["ki"]),
            hp(st_p["mk"]), hp(st_p["mv"]),
            kv(st_s["k_sb"]), kv(st_s["v_sb"]), kv(st_s["k_d"]), kv(st_s["v_d"]), hs(st_s["ki"]))
```

```python
import functools
import math

import numpy as np
import jax
import jax.numpy as jnp
from jax import lax
from jax.experimental import pallas as pl
from jax.experimental.pallas import tpu as pltpu

HEAD_DIM = 128
SB_HEADS = 6
DSA_HEADS = 6
MEM_HEADS = 4
SB_WIDTH = SB_HEADS * HEAD_DIM
DSA_WIDTH = DSA_HEADS * HEAD_DIM
MEM_WIDTH = MEM_HEADS * HEAD_DIM
IDX_HEADS = 16
IDX_DIM = 64
IDX_WIDTH = IDX_HEADS * IDX_DIM
PAGE_SIZE = 128
TOPK_MAX = 256
Q_BLOCK = 128
ROPE_THETA = 10000.0
NORM_EPS = 1e-6
ATTN_SCALE = HEAD_DIM ** -0.5
LOG2E = math.log2(math.e)
ATTN_SCALE2 = ATTN_SCALE * LOG2E
IDX_SCALE = (IDX_HEADS * IDX_DIM) ** -0.5

SB_DEAD_LOG2 = -160.0
LANES = 128
NEG_BIAS = -1e30
INT32_MIN = -2 ** 31
KEY_NEG_INF = -2139095041
VMEM_LIMIT = 56 * 2 ** 20

_F32 = jnp.float32
_BF16 = jnp.bfloat16
_I32 = jnp.int32
_NT = (((1,), (1,)), ((), ()))


def _params(*sem):
    return pltpu.CompilerParams(dimension_semantics=sem, vmem_limit_bytes=VMEM_LIMIT)


def _sigmoid(x):
    return 1.0 / (1.0 + jnp.exp(-x))


def _silu(x):
    return x * _sigmoid(x)


def _softplus2(t):
    return jnp.maximum(t, 0.0) + jnp.log2(1.0 + jnp.exp2(-jnp.abs(t)))


def _split_bf16(x):
    hi = x.astype(_BF16)
    lo = (x - hi.astype(_F32)).astype(_BF16)
    return hi, lo


def _sortable_key(s):
    b = lax.bitcast_convert_type(s, _I32)
    return b ^ ((b >> 31) & 0x7FFFFFFF)


def _largest_divisor(n, candidates):
    for c in candidates:
        if n % c == 0:
            return c
    raise ValueError(f"no tile in {candidates} divides {n}")


def _rms_kernel(x_ref, g_ref, o_ref):
    x = x_ref[...]
    y = x * lax.rsqrt(jnp.mean(x * x, axis=-1, keepdims=True) + NORM_EPS)
    o_ref[...] = (y * g_ref[...]).astype(o_ref.dtype)


def _rmsnorm(x, g, out_dtype):
    n, d = x.shape
    tm = _largest_divisor(n, (512, 256, 128, 64))
    return pl.pallas_call(
        _rms_kernel,
        grid=(n // tm,),
        in_specs=[pl.BlockSpec((tm, d), lambda i: (i, 0)),
                  pl.BlockSpec((1, d), lambda i: (0, 0))],
        out_specs=pl.BlockSpec((tm, d), lambda i: (i, 0)),
        out_shape=jax.ShapeDtypeStruct((n, d), out_dtype),
        compiler_params=_params("parallel"),
        name="rmsnorm",
    )(x, g.reshape(1, d))


def _proj_kernel(*refs, shifts, kinds, n_alias):
    h_ref, w_ref = refs[0], refs[1]
    n_tab = 1 + len(shifts) if shifts else 0
    tabs = refs[2:2 + n_tab]
    outs = refs[2 + n_tab + n_alias:]
    z = jnp.dot(h_ref[...], w_ref[...], preferred_element_type=_F32)
    tn = z.shape[1]
    for g in range(tn // LANES):
        zg = z[:, g * LANES:(g + 1) * LANES]
        if shifts:
            r = zg * tabs[0][...]
            for t, sh in enumerate(shifts):
                r = r + pltpu.roll(zg, sh, 1) * tabs[1 + t][...]
            zg = r
        for kind, o_ref in zip(kinds, outs):
            if kind == "flat":
                o_ref[:, g * LANES:(g + 1) * LANES] = zg.astype(o_ref.dtype)
            elif kind == "heads":
                o_ref[:, g, :] = zg
            elif kind == "heads_major":
                if len(o_ref.shape) == 3:
                    o_ref[g] = zg
                else:
                    gr = o_ref.shape[2]
                    for b in range(o_ref.shape[0]):
                        o_ref[b, g] = zg[b * gr:(b + 1) * gr]
            else:
                o_ref[...] = zg[:, :o_ref.shape[1]].astype(o_ref.dtype)


def _proj(h, w, layer, col0, width, outs, rope=None, name="proj"):
    n, d = h.shape
    tm = _largest_divisor(n, (1024, 512, 256, 128, 64))
    tn = _largest_divisor(width, (1024, 768, 512, 256, 128))
    assert col0 % tn == 0
    cb = col0 // tn
    shifts, tables = rope if rope else ((), ())
    in_specs = [pl.BlockSpec((tm, d), lambda j, i: (i, 0)),
                pl.BlockSpec((None, d, tn), lambda j, i: (layer, 0, cb + j))]
    operands = [h, w]
    for t in tables:
        assert t.shape == (n, LANES)
        in_specs.append(pl.BlockSpec((tm, LANES), lambda j, i: (i, 0)))
        operands.append(t)
    kinds, out_specs, out_shape, aliases = [], [], [], {}
    prevs = []
    for req in outs:
        kind = req[0]
        if kind == "flat":
            out_specs.append(pl.BlockSpec((tm, tn), lambda j, i: (i, j)))
            out_shape.append(jax.ShapeDtypeStruct((n, width), req[1]))
        elif kind == "lanes":
            assert width == LANES
            out_specs.append(pl.BlockSpec((tm, req[1]), lambda j, i: (i, 0)))
            out_shape.append(jax.ShapeDtypeStruct((n, req[1]), req[2]))
        elif kind == "heads":
            assert tn == width
            nh = width // LANES
            out_specs.append(pl.BlockSpec((None, tm, nh, LANES), lambda j, i: (layer, i, 0, 0)))
            out_shape.append(jax.ShapeDtypeStruct((req[2], n, nh, LANES), _F32))
            prevs.append((len(kinds), req[1]))
        elif kind == "heads_major":
            assert tn == width
            nh = width // LANES
            gr = req[3]
            if tm <= gr:
                per = gr // tm
                out_specs.append(pl.BlockSpec((None, None, nh, tm, LANES),
                                              lambda j, i, per=per: (layer, i // per, 0, i % per, 0)))
            else:
                out_specs.append(pl.BlockSpec((None, tm // gr, nh, gr, LANES), lambda j, i: (layer, i, 0, 0, 0)))
            out_shape.append(jax.ShapeDtypeStruct((req[2], n // gr, nh, gr, LANES), _F32))
            prevs.append((len(kinds), req[1]))
        elif kind == "lanes_stacked":
            assert width == LANES
            out_specs.append(pl.BlockSpec((None, tm, req[1]), lambda j, i: (layer, i, 0)))
            out_shape.append(jax.ShapeDtypeStruct((req[3], n, req[1]), _F32))
            prevs.append((len(kinds), req[2]))
        else:
            raise ValueError(kind)
        kinds.append(kind)
    n_alias = 0
    for out_idx, prev in prevs:
        if prev is not None:
            aliases[len(operands)] = out_idx
            in_specs.append(pl.BlockSpec(memory_space=pl.ANY))
            operands.append(prev)
            n_alias += 1
    return pl.pallas_call(
        functools.partial(_proj_kernel, shifts=tuple(shifts), kinds=tuple(kinds), n_alias=n_alias),
        grid=(width // tn, n // tm),
        in_specs=in_specs,
        out_specs=out_specs,
        out_shape=out_shape,
        input_output_aliases=aliases,
        compiler_params=_params("parallel", "parallel"),
        name=name,
    )(*operands)


def _cumsum_matrix(t):
    r = np.arange(2 * t) % t
    c = np.arange(2 * t)
    m = np.where(c[None, :] < t, r[:, None] > c[None, :], True)
    return jnp.asarray(m, dtype=_BF16)


def _sb_prompt_kernel(q_ref, k_ref, v_ref, z_ref, u_ref, o_ref, carry_ref, acc_ref, *, tq):
    i = pl.program_id(1)
    heads = SB_HEADS
    hsl = [slice(h * HEAD_DIM, (h + 1) * HEAD_DIM) for h in range(heads)]
    rsl = [slice(h * tq, (h + 1) * tq) for h in range(heads)]

    def key_tiles(off, nt, before):
        n = heads * tq
        s = jnp.concatenate(
            [lax.dot_general(q_ref[:, hsl[h]], k_ref[pl.ds(off + (nt - 1 - j) * tq, tq), hsl[h]], _NT,
                             preferred_element_type=_F32)
             for j in range(nt) for h in range(heads)], axis=0) * ATTN_SCALE2
        sp = _softplus2(s)
        lk = -sp
        if before is not None:
            lk = jnp.where(before, lk, 0.0)
        hi, lo = _split_bf16(lk)
        cr = jnp.dot(jnp.concatenate([hi, lo], axis=1), u_ref[...], preferred_element_type=_F32)
        carry = carry_ref[...]
        for j in range(nt):
            r = slice(j * n, (j + 1) * n)
            a = jnp.exp2(s[r] - sp[r] + carry + cr[r, :tq])
            if before is not None:
                a = jnp.where(before, a, 0.0)
            carry = carry + cr[r, tq:]
            ab = a.astype(_BF16)
            for h in range(heads):
                acc_ref[rsl[h], :] += jnp.dot(ab[rsl[h]], v_ref[pl.ds(off + (nt - 1 - j) * tq, tq), hsl[h]],
                                              preferred_element_type=_F32)
        carry_ref[...] = carry

    carry_ref[...] = jnp.zeros_like(carry_ref)
    acc_ref[...] = jnp.zeros_like(acc_ref)
    row = lax.broadcasted_iota(_I32, (heads * tq, tq), 0) & (tq - 1)
    col = lax.broadcasted_iota(_I32, (heads * tq, tq), 1)
    key_tiles(pl.multiple_of(i * tq, tq), 1, col < row)

    def live():
        return jnp.max(carry_ref[...]) > SB_DEAD_LOG2

    def body(c):
        key_tiles(pl.multiple_of((i - 2 - 2 * c[0]) * tq, tq), 2, None)
        return c[0] + 1, live()

    _, alive = lax.while_loop(lambda c: jnp.logical_and(c[0] < i // 2, c[1]), body, (jnp.int32(0), live()))

    @pl.when(jnp.logical_and(i % 2 == 1, alive))
    def _():
        key_tiles(0, 1, None)
    for h in range(heads):
        o_ref[:, hsl[h]] = (acc_ref[rsl[h], :] * _silu(z_ref[:, hsl[h]])).astype(o_ref.dtype)


def _sb_prompt(q, k, v, z, b, t):
    tq = Q_BLOCK
    nq = t // tq
    w = SB_WIDTH
    u = _cumsum_matrix(tq)
    return pl.pallas_call(
        functools.partial(_sb_prompt_kernel, tq=tq),
        grid=(b, nq),
        in_specs=[pl.BlockSpec((tq, w), lambda bi, i: (bi * nq + i, 0)),
                  pl.BlockSpec((t, w), lambda bi, i: (bi, 0)),
                  pl.BlockSpec((t, w), lambda bi, i: (bi, 0)),
                  pl.BlockSpec((tq, w), lambda bi, i: (bi * nq + i, 0)),
                  pl.BlockSpec((2 * tq, 2 * tq), lambda bi, i: (0, 0))],
        out_specs=pl.BlockSpec((tq, w), lambda bi, i: (bi * nq + i, 0)),
        out_shape=jax.ShapeDtypeStruct(q.shape, _BF16),
        scratch_shapes=[pltpu.VMEM((SB_HEADS * tq, tq), _F32), pltpu.VMEM((SB_HEADS * tq, HEAD_DIM), _F32)],
        compiler_params=_params("parallel", "arbitrary"),
        name="sb_prompt",
    )(q, k, v, z, u)


def _select_to_bias(key_ref, bias_ref, nch, ch, kk):
    rows = key_ref.shape[0]

    def count(pred):
        def body(c, part):
            off = pl.multiple_of(c * ch, ch)
            m = jnp.where(pred(key_ref[:, pl.ds(off, ch)], off), 1, 0).astype(_I32)
            for g in range(ch // LANES):
                part = part + m[:, g * LANES:(g + 1) * LANES]
            return part
        part = lax.fori_loop(0, nch, body, jnp.zeros((rows, LANES), _I32))
        return jnp.sum(part, axis=1, keepdims=True)

    def bit_body(it, u):
        cand_u = u | lax.shift_left(jnp.int32(1), 31 - it)
        cand = cand_u ^ jnp.int32(INT32_MIN)
        cnt = count(lambda kc, off: kc >= cand)
        return jnp.where(cnt >= kk, cand_u, u)

    u = lax.fori_loop(0, 32, bit_body, jnp.zeros((rows, 1), _I32))
    thr = u ^ jnp.int32(INT32_MIN)
    cnt_gt = count(lambda kc, off: kc > thr)
    cnt_ge = count(lambda kc, off: kc >= thr)
    need = kk - cnt_gt
    excess = jnp.where((cnt_ge > kk) & (thr > KEY_NEG_INF), 1, 0).astype(_I32)
    nbits = max(1, int(math.ceil(math.log2(key_ref.shape[1] + 1))))

    def tie_cut(_):
        def q_body(it, q):
            cand = q | lax.shift_left(jnp.int32(1), nbits - 1 - it)

            def pred(kc, off):
                idx = off + lax.broadcasted_iota(_I32, kc.shape, 1)
                return (kc == thr) & (idx < cand)
            return jnp.where(count(pred) < need, cand, q)
        return lax.fori_loop(0, nbits, q_body, jnp.zeros((rows, 1), _I32))

    def no_cut(_):
        return jnp.full((rows, 1), 2 ** 30, _I32)

    qcut = lax.cond(jnp.max(excess) > 0, tie_cut, no_cut, 0)

    def write(c, _):
        off = pl.multiple_of(c * ch, ch)
        kc = key_ref[:, pl.ds(off, ch)]
        idx = off + lax.broadcasted_iota(_I32, kc.shape, 1)
        sel = (kc > thr) | ((kc == thr) & (idx <= qcut))
        ok = sel & (kc > KEY_NEG_INF)
        bias_ref[:, pl.ds(off, ch)] = jnp.where(ok, 0.0, NEG_BIAS).astype(_F32)
        return 0

    lax.fori_loop(0, nch, write, 0)


def _idx_prompt_kernel(qi_ref, wi_ref, ki_ref, bias_ref, key_ref, *, tq, ch, kk):
    i = pl.program_id(1)
    nch = (i * tq + tq + ch - 1) // ch
    qi = qi_ref[...]
    wi = wi_ref[...]
    rowpos = i * tq + lax.broadcasted_iota(_I32, (tq, ch), 0)

    def score(c, _):
        off = pl.multiple_of(c * ch, ch)
        kc = ki_ref[pl.ds(off, ch), :]
        acc = jnp.zeros((tq, ch), _F32)
        for h in range(IDX_HEADS):
            r = lax.dot_general(qi[:, h * IDX_DIM:(h + 1) * IDX_DIM], kc, _NT,
                                preferred_element_type=_F32)
            acc = acc + jnp.maximum(r, 0.0) * wi[:, IDX_DIM + h:IDX_DIM + h + 1]
        sc = acc * IDX_SCALE
        sc = jnp.where(sc == 0.0, 0.0, sc)
        colpos = off + lax.broadcasted_iota(_I32, (tq, ch), 1)
        sc = jnp.where(colpos <= rowpos, sc, -jnp.inf)
        key_ref[:, pl.ds(off, ch)] = _sortable_key(sc)
        return 0

    lax.fori_loop(0, nch, score, 0)
    bias_ref[...] = jnp.full(bias_ref.shape, NEG_BIAS, _F32)
    _select_to_bias(key_ref, bias_ref, nch, ch, kk)


def _idx_prompt(qi, kiwi, ki, b, t, kk):
    tq = Q_BLOCK
    nq = t // tq
    ch = _largest_divisor(t, (512, 256, 128))
    return pl.pallas_call(
        functools.partial(_idx_prompt_kernel, tq=tq, ch=ch, kk=kk),
        grid=(b, nq),
        in_specs=[pl.BlockSpec((tq, IDX_WIDTH), lambda bi, i: (bi * nq + i, 0)),
                  pl.BlockSpec((tq, LANES), lambda bi, i: (bi * nq + i, 0)),
                  pl.BlockSpec((t, IDX_DIM), lambda bi, i: (bi, 0))],
        out_specs=pl.BlockSpec((tq, t), lambda bi, i: (bi * nq + i, 0)),
        out_shape=jax.ShapeDtypeStruct((b * t, t), _F32),
        scratch_shapes=[pltpu.VMEM((tq, t), _I32)],
        compiler_params=_params("parallel", "arbitrary"),
        name="idx_select_prompt",
    )(qi, kiwi, ki)


def _dsa_prompt_kernel(q_ref, k_ref, v_ref, bias_ref, z_ref, o_ref, m_ref, l_ref, acc_ref, *, tq, ch):
    i = pl.program_id(1)
    nch = (i * tq + tq + ch - 1) // ch
    heads = DSA_HEADS
    hsl = [slice(h * HEAD_DIM, (h + 1) * HEAD_DIM) for h in range(heads)]
    rsl = [slice(h * tq, (h + 1) * tq) for h in range(heads)]
    m_ref[...] = jnp.full(m_ref.shape, NEG_BIAS, _F32)
    l_ref[...] = jnp.zeros_like(l_ref)
    acc_ref[...] = jnp.zeros_like(acc_ref)
    ones = jnp.ones((ch, LANES), _BF16)

    def body(c, _):
        off = pl.multiple_of(c * ch, ch)
        bias = bias_ref[:, pl.ds(off, ch)]
        s = jnp.concatenate(
            [lax.dot_general(q_ref[:, hsl[h]], k_ref[pl.ds(off, ch), hsl[h]], _NT, preferred_element_type=_F32)
             for h in range(heads)], axis=0) * (ATTN_SCALE * LOG2E) + jnp.concatenate([bias] * heads, axis=0)
        m_old = m_ref[...]
        m_new = jnp.maximum(m_old, jnp.max(s, axis=1, keepdims=True))
        alpha = jnp.exp2(m_old - m_new)
        pb = jnp.exp2(s - m_new).astype(_BF16)
        l_ref[...] = l_ref[...] * alpha + jnp.dot(pb, ones, preferred_element_type=_F32)
        for h in range(heads):
            acc_ref[rsl[h], :] = acc_ref[rsl[h], :] * alpha[rsl[h]] + jnp.dot(
                pb[rsl[h]], v_ref[pl.ds(off, ch), hsl[h]], preferred_element_type=_F32)
        m_ref[...] = m_new
        return 0

    lax.fori_loop(0, nch, body, 0)
    for h in range(heads):
        o_ref[:, hsl[h]] = (acc_ref[rsl[h], :] / l_ref[rsl[h], :] * _silu(z_ref[:, hsl[h]])).astype(o_ref.dtype)


def _dsa_prompt(q, k, v, bias, z, b, t):
    tq = Q_BLOCK
    nq = t // tq
    ch = _largest_divisor(t, (512, 256, 128))
    w = DSA_WIDTH
    return pl.pallas_call(
        functools.partial(_dsa_prompt_kernel, tq=tq, ch=ch),
        grid=(b, nq),
        in_specs=[pl.BlockSpec((tq, w), lambda bi, i: (bi * nq + i, 0)),
                  pl.BlockSpec((t, w), lambda bi, i: (bi, 0)),
                  pl.BlockSpec((t, w), lambda bi, i: (bi, 0)),
                  pl.BlockSpec((tq, t), lambda bi, i: (bi * nq + i, 0)),
                  pl.BlockSpec((tq, w), lambda bi, i: (bi * nq + i, 0))],
        out_specs=pl.BlockSpec((tq, w), lambda bi, i: (bi * nq + i, 0)),
        out_shape=jax.ShapeDtypeStruct(q.shape, _BF16),
        scratch_shapes=[pltpu.VMEM((DSA_HEADS * tq, 1), _F32), pltpu.VMEM((DSA_HEADS * tq, LANES), _F32),
                        pltpu.VMEM((DSA_HEADS * tq, HEAD_DIM), _F32)],
        compiler_params=_params("parallel", "arbitrary"),
        name="dsa_prompt",
    )(q, k, v, bias, z)


def _mem_kernel(q_ref, mk_ref, mv_ref, z_ref, o_ref):
    tm = q_ref.shape[0]
    pad = (-tm) % 16
    by_head = len(mk_ref.shape) == 3
    for h in range(MEM_HEADS):
        hs = slice(h * HEAD_DIM, (h + 1) * HEAD_DIM)
        mk = mk_ref[:, h, :] if by_head else mk_ref[:, hs]
        mv = mv_ref[:, h, :] if by_head else mv_ref[:, hs]
        qh = q_ref[:, hs].astype(_F32)
        if pad:
            qh = jnp.concatenate([qh, jnp.zeros((pad, HEAD_DIM), _F32)], axis=0)
        s = lax.dot_general(qh.astype(_BF16), mk.astype(_BF16), _NT, preferred_element_type=_F32) * ATTN_SCALE2
        p = jnp.exp2(s - jnp.max(s, axis=1, keepdims=True))
        p = p / jnp.sum(p, axis=1, keepdims=True)
        o = jnp.dot(p.astype(_BF16), mv.astype(_BF16), preferred_element_type=_F32)
        o_ref[:, hs] = (o[:tm] * _silu(z_ref[:, hs])).astype(o_ref.dtype)


def _mem_attend(q, mk, mv, z, layer=None):
    g, r, w = q.shape
    tm = _largest_divisor(r, (512, 256, 128, 64, 8))
    if layer is None:
        m = mk.shape[1]
        mem = pl.BlockSpec((None, m, w), lambda gi, i: (gi, 0, 0))
    else:
        m = mk.shape[2]
        mem = pl.BlockSpec((None, None, m, MEM_HEADS, HEAD_DIM), lambda gi, i: (layer, gi, 0, 0, 0))
    return pl.pallas_call(
        _mem_kernel,
        grid=(g, r // tm),
        in_specs=[pl.BlockSpec((None, tm, w), lambda gi, i: (gi, i, 0)),
                  mem, mem,
                  pl.BlockSpec((None, tm, w), lambda gi, i: (gi, i, 0))],
        out_specs=pl.BlockSpec((None, tm, w), lambda gi, i: (gi, i, 0)),
        out_shape=jax.ShapeDtypeStruct((g, r, w), _BF16),
        compiler_params=_params("parallel", "parallel"),
        name="mem_attend",
    )(q, mk, mv, z)


def _mix_kernel(gs_ref, gd_ref, gm_ref, g1_ref, g2_ref, g3_ref, w1_ref, w2_ref, w3_ref, o_ref):
    a = jnp.dot(gs_ref[...], w1_ref[...], preferred_element_type=_F32)
    c = jnp.dot(gd_ref[...], w2_ref[...], preferred_element_type=_F32)
    m = jnp.dot(gm_ref[...], w3_ref[...], preferred_element_type=_F32)
    mixed = _sigmoid(g1_ref[...]) * a + _sigmoid(g2_ref[...]) * c + _sigmoid(g3_ref[...]) * m
    o_ref[...] = mixed.astype(o_ref.dtype)


def _mix(gs, gd, gm, gates, w1, w2, w3, layer):
    n = gs.shape[0]
    d = w1.shape[2]
    tm = _largest_divisor(n, (256, 128, 64))
    row = lambda w: pl.BlockSpec((tm, w), lambda i: (i, 0))
    wspec = lambda w: pl.BlockSpec((None, w.shape[1], d), lambda i: (layer, 0, 0))
    return pl.pallas_call(
        _mix_kernel,
        grid=(n // tm,),
        in_specs=[row(SB_WIDTH), row(DSA_WIDTH), row(MEM_WIDTH),
                  pl.BlockSpec((tm, d), lambda i: (i, 0)),
                  pl.BlockSpec((tm, d), lambda i: (i, 1)),
                  pl.BlockSpec((tm, d), lambda i: (i, 2)),
                  wspec(w1), wspec(w2), wspec(w3)],
        out_specs=pl.BlockSpec((tm, d), lambda i: (i, 0)),
        out_shape=jax.ShapeDtypeStruct((n, d), _BF16),
        compiler_params=_params("parallel"),
        name="mix",
    )(gs, gd, gm, gates, gates, gates, w1, w2, w3)


def _out_kernel(x_ref, m_ref, w_ref, g_ref, xo_ref, ho_ref):
    xn = x_ref[...] + jnp.dot(m_ref[...], w_ref[...], preferred_element_type=_F32)
    xo_ref[...] = xn
    y = xn * lax.rsqrt(jnp.mean(xn * xn, axis=-1, keepdims=True) + NORM_EPS)
    ho_ref[...] = (y * g_ref[...]).astype(ho_ref.dtype)


def _out_proj(x, mixed, w, layer, g, h_dtype):
    n, d = x.shape
    tm = _largest_divisor(n, (512, 256, 128, 64))
    return pl.pallas_call(
        _out_kernel,
        grid=(n // tm,),
        in_specs=[pl.BlockSpec((tm, d), lambda i: (i, 0)),
                  pl.BlockSpec((tm, d), lambda i: (i, 0)),
                  pl.BlockSpec((None, d, d), lambda i: (layer, 0, 0)),
                  pl.BlockSpec((1, d), lambda i: (0, 0))],
        out_specs=[pl.BlockSpec((tm, d), lambda i: (i, 0)),
                   pl.BlockSpec((tm, d), lambda i: (i, 0))],
        out_shape=[jax.ShapeDtypeStruct((n, d), _F32), jax.ShapeDtypeStruct((n, d), h_dtype)],
        compiler_params=_params("parallel"),
        name="out_proj",
    )(x, mixed, w, g.reshape(1, d))


def _head_rows(x, ds, h):
    return x[h * ds:(h + 1) * ds]


def _stack_scores(qs, key_heads, ds):
    parts = []
    for h, kh in enumerate(key_heads):
        r = lax.dot_general(qs, kh, _NT, preferred_element_type=_F32)
        parts.append(_head_rows(r, ds, h))
    return jnp.concatenate(parts, axis=0) * ATTN_SCALE2


def _stack_pv(p, val_heads, ds):
    pb = p.astype(_BF16)
    parts = []
    for h, vh in enumerate(val_heads):
        r = jnp.dot(pb, vh, preferred_element_type=_F32)
        parts.append(_head_rows(r, ds, h))
    return jnp.concatenate(parts, axis=0)


def _pad_rows(x, rows):
    return jnp.concatenate([x, jnp.zeros((rows - x.shape[0], x.shape[1]), x.dtype)], axis=0)


def _page_heads(ref):
    return [ref[h].astype(_BF16) for h in range(ref.shape[0])]


def _new_heads(ref, heads):
    x = _pad_rows(ref[...], PAGE_SIZE).astype(_BF16)
    return [x[:, h * HEAD_DIM:(h + 1) * HEAD_DIM] for h in range(heads)]


def _pages_per_step(n_pages, candidates):
    return _largest_divisor(n_pages, candidates)


def _sb_sample_kernel(pt_ref, q_ref, kn_ref, vn_ref, *rest, ds, n_steps, pp):
    kc, vc = rest[:pp], rest[pp:2 * pp]
    z_ref, u_ref, o_ref, carry_ref, acc_ref = rest[2 * pp:]
    p = pl.program_id(1)
    rows = SB_HEADS * ds
    qs = q_ref[...]

    def page(key_heads, val_heads, before, carry, acc):
        s = _stack_scores(qs, key_heads, ds)
        sp = _softplus2(s)
        lk = -sp
        if before is not None:
            lk = jnp.where(before, lk, 0.0)
        hi, lo = _split_bf16(lk)
        cr = jnp.dot(jnp.concatenate([hi, lo], axis=1), u_ref[...], preferred_element_type=_F32)
        a = jnp.exp2(s - sp + carry + cr[:, :PAGE_SIZE])
        if before is not None:
            a = jnp.where(before, a, 0.0)
        return carry + cr[:, PAGE_SIZE:], acc + _stack_pv(a, val_heads, ds)

    @pl.when(p == 0)
    def _():
        qrow = lax.broadcasted_iota(_I32, (rows, PAGE_SIZE), 0) % ds
        col = lax.broadcasted_iota(_I32, (rows, PAGE_SIZE), 1)
        carry, acc = page(_new_heads(kn_ref, SB_HEADS), _new_heads(vn_ref, SB_HEADS), col < qrow,
                          jnp.zeros((rows, PAGE_SIZE), _F32), jnp.zeros((rows, HEAD_DIM), _F32))
        carry_ref[...] = carry
        acc_ref[...] = acc

    @pl.when(jnp.max(carry_ref[...]) > SB_DEAD_LOG2)
    def _():
        carry, acc = carry_ref[...], acc_ref[...]
        s = jnp.concatenate([_stack_scores(qs, _page_heads(kc[k]), ds) for k in range(pp)], axis=0)
        sp = _softplus2(s)
        hi, lo = _split_bf16(-sp)
        cr = jnp.dot(jnp.concatenate([hi, lo], axis=1), u_ref[...], preferred_element_type=_F32)
        weights = []
        for k in range(pp):
            r = slice(k * rows, (k + 1) * rows)
            weights.append(jnp.exp2(s[r] - sp[r] + carry + cr[r, :PAGE_SIZE]))
            carry = carry + cr[r, PAGE_SIZE:]
        for k in range(pp):
            acc = acc + _stack_pv(weights[k], _page_heads(vc[k]), ds)
        carry_ref[...] = carry
        acc_ref[...] = acc

    @pl.when(p == n_steps - 1)
    def _():
        o_ref[...] = (acc_ref[...] * _silu(z_ref[...])).astype(o_ref.dtype)


def _sb_sample(qs, kn, vn, cache_k, cache_v, zs, page_table, layer):
    db, rows, _ = qs.shape
    ds = rows // SB_HEADS
    n_pages = page_table.shape[1]
    pp = _pages_per_step(n_pages, (8, 4, 2, 1))
    n_steps = n_pages // pp
    w = SB_WIDTH
    u = _cumsum_matrix(PAGE_SIZE)
    seq = lambda r, c: pl.BlockSpec((None, r, c), lambda b, p, pt: (b, 0, 0))
    pages = [pl.BlockSpec((None, None, SB_HEADS, PAGE_SIZE, HEAD_DIM),
                          lambda b, p, pt, k=k: (layer, pt[b, n_pages - 1 - (p * pp + k)], 0, 0, 0))
             for k in range(pp)]
    return pl.pallas_call(
        functools.partial(_sb_sample_kernel, ds=ds, n_steps=n_steps, pp=pp),
        grid_spec=pltpu.PrefetchScalarGridSpec(
            num_scalar_prefetch=1,
            grid=(db, n_steps),
            in_specs=[seq(rows, HEAD_DIM), seq(ds, w), seq(ds, w)] + pages + pages
                     + [seq(rows, HEAD_DIM), pl.BlockSpec((2 * PAGE_SIZE, 2 * PAGE_SIZE), lambda b, p, pt: (0, 0))],
            out_specs=seq(rows, HEAD_DIM),
            scratch_shapes=[pltpu.VMEM((rows, PAGE_SIZE), _F32), pltpu.VMEM((rows, HEAD_DIM), _F32)]),
        out_shape=jax.ShapeDtypeStruct(qs.shape, _BF16),
        compiler_params=_params("parallel", "arbitrary"),
        name="sb_sample",
    )(page_table, qs, kn, vn, *([cache_k] * pp), *([cache_v] * pp), zs, u)


def _idx_sample_kernel(pt_ref, qi_ref, wi_ref, kn_ref, *rest, ds, pp):
    kc = rest[:pp]
    oc_ref, on_ref = rest[pp:]
    p = pl.program_id(1)

    def scores(r):
        r = jnp.maximum(r, 0.0) * wi_ref[...]
        sc = _head_rows(r, ds, 0)
        for h in range(1, IDX_HEADS):
            sc = sc + _head_rows(r, ds, h)
        sc = sc * IDX_SCALE
        return jnp.where(sc == 0.0, 0.0, sc)

    for k in range(pp):
        r = jnp.dot(qi_ref[...], kc[k][...].astype(_BF16), preferred_element_type=_F32)
        oc_ref[:, k * PAGE_SIZE:(k + 1) * PAGE_SIZE] = _sortable_key(scores(r))

    @pl.when(p == 0)
    def _():
        kn = _pad_rows(kn_ref[...], PAGE_SIZE).astype(_BF16)
        sc = scores(lax.dot_general(qi_ref[...], kn, _NT, preferred_element_type=_F32))
        s_row = lax.broadcasted_iota(_I32, (ds, PAGE_SIZE), 0)
        col = lax.broadcasted_iota(_I32, (ds, PAGE_SIZE), 1)
        on_ref[...] = _sortable_key(jnp.where(col <= s_row, sc, -jnp.inf))


def _idx_sample(qi_t, wi_col, ki_new, cache_ik, page_table, layer):
    db, rows, _ = qi_t.shape
    ds = rows // IDX_HEADS
    n_pages = page_table.shape[1]
    pp = _pages_per_step(n_pages, (16, 8, 4, 2, 1))
    seq = lambda r, c: pl.BlockSpec((None, r, c), lambda b, p, pt: (b, 0, 0))
    pages = [pl.BlockSpec((None, None, IDX_DIM, PAGE_SIZE),
                          lambda b, p, pt, k=k: (layer, pt[b, p * pp + k], 0, 0)) for k in range(pp)]
    return pl.pallas_call(
        functools.partial(_idx_sample_kernel, ds=ds, pp=pp),
        grid_spec=pltpu.PrefetchScalarGridSpec(
            num_scalar_prefetch=1,
            grid=(db, n_pages // pp),
            in_specs=[seq(rows, IDX_DIM), seq(rows, 1), seq(ds, IDX_DIM)] + pages,
            out_specs=[pl.BlockSpec((None, ds, pp * PAGE_SIZE), lambda b, p, pt: (b, 0, p)),
                       seq(ds, PAGE_SIZE)]),
        out_shape=[jax.ShapeDtypeStruct((db, ds, n_pages * PAGE_SIZE), _I32),
                   jax.ShapeDtypeStruct((db, ds, PAGE_SIZE), _I32)],
        compiler_params=_params("parallel", "arbitrary"),
        name="idx_scores_sample",
    )(page_table, qi_t, wi_col, ki_new, *([cache_ik] * pp))


def _select_kernel(kc_ref, kn_ref, bc_ref, bn_ref, key_ref, bias_ref, *, ch, kk):
    n_past = kc_ref.shape[1]
    key_ref[:, :n_past] = kc_ref[...]
    key_ref[:, n_past:] = kn_ref[...]
    _select_to_bias(key_ref, bias_ref, key_ref.shape[1] // ch, ch, kk)
    bc_ref[...] = bias_ref[:, :n_past]
    bn_ref[...] = bias_ref[:, n_past:]


def _select_sample(keys_cache, keys_new, kk):
    rows, n_past = keys_cache.shape
    s = n_past + keys_new.shape[1]
    ch = _largest_divisor(s, (512, 384, 256, 128))
    full = lambda c: pl.BlockSpec((rows, c), lambda i: (0, 0))
    return pl.pallas_call(
        functools.partial(_select_kernel, ch=ch, kk=kk),
        grid=(1,),
        in_specs=[full(n_past), full(PAGE_SIZE)],
        out_specs=[full(n_past), full(PAGE_SIZE)],
        out_shape=[jax.ShapeDtypeStruct((rows, n_past), _F32), jax.ShapeDtypeStruct((rows, PAGE_SIZE), _F32)],
        scratch_shapes=[pltpu.VMEM((rows, s), _I32), pltpu.VMEM((rows, s), _F32)],
        compiler_params=_params("arbitrary"),
        name="select_sample",
    )(keys_cache, keys_new)


def _dsa_sample_kernel(pt_ref, q_ref, kn_ref, vn_ref, *rest, ds, n_steps, pp):
    kc, vc = rest[:pp], rest[pp:2 * pp]
    bn_ref, bc_ref, z_ref, o_ref, m_ref, l_ref, acc_ref = rest[2 * pp:]
    p = pl.program_id(1)
    qs = q_ref[...]

    def step(key_pages, val_pages, bias, m_old, l_old, acc_old):
        s = jnp.concatenate([_stack_scores(qs, kh, ds) for kh in key_pages], axis=1)
        s = s + jnp.concatenate([bias] * DSA_HEADS, axis=0)
        m_new = jnp.maximum(m_old, jnp.max(s, axis=1, keepdims=True))
        alpha = jnp.exp2(m_old - m_new)
        pr = jnp.exp2(s - m_new)
        pv = _stack_pv(pr[:, :PAGE_SIZE], val_pages[0], ds)
        for k in range(1, len(val_pages)):
            pv = pv + _stack_pv(pr[:, k * PAGE_SIZE:(k + 1) * PAGE_SIZE], val_pages[k], ds)
        return m_new, l_old * alpha + jnp.sum(pr, axis=1, keepdims=True), acc_old * alpha + pv

    @pl.when(p == 0)
    def _():
        rows = DSA_HEADS * ds
        m, l, acc = step([_new_heads(kn_ref, DSA_HEADS)], [_new_heads(vn_ref, DSA_HEADS)], bn_ref[...],
                         jnp.full((rows, 1), NEG_BIAS, _F32), jnp.zeros((rows, 1), _F32),
                         jnp.zeros((rows, HEAD_DIM), _F32))
        m_ref[...] = m
        l_ref[...] = l
        acc_ref[...] = acc

    m, l, acc = step([_page_heads(r) for r in kc], [_page_heads(r) for r in vc], bc_ref[...],
                     m_ref[...], l_ref[...], acc_ref[...])
    m_ref[...] = m
    l_ref[...] = l
    acc_ref[...] = acc

    @pl.when(p == n_steps - 1)
    def _():
        o_ref[...] = (acc / l * _silu(z_ref[...])).astype(o_ref.dtype)


def _dsa_sample(qs, kn, vn, cache_k, cache_v, bias_cache, bias_new, zs, page_table, layer):
    db, rows, _ = qs.shape
    ds = rows // DSA_HEADS
    n_pages = page_table.shape[1]
    pp = _pages_per_step(n_pages, (8, 4, 2, 1))
    n_steps = n_pages // pp
    w = DSA_WIDTH
    seq = lambda r, c: pl.BlockSpec((None, r, c), lambda b, p, pt: (b, 0, 0))
    pages = [pl.BlockSpec((None, None, DSA_HEADS, PAGE_SIZE, HEAD_DIM),
                          lambda b, p, pt, k=k: (layer, pt[b, p * pp + k], 0, 0, 0)) for k in range(pp)]
    return pl.pallas_call(
        functools.partial(_dsa_sample_kernel, ds=ds, n_steps=n_steps, pp=pp),
        grid_spec=pltpu.PrefetchScalarGridSpec(
            num_scalar_prefetch=1,
            grid=(db, n_steps),
            in_specs=[seq(rows, HEAD_DIM), seq(ds, w), seq(ds, w)] + pages + pages
                     + [seq(ds, PAGE_SIZE),
                        pl.BlockSpec((None, ds, pp * PAGE_SIZE), lambda b, p, pt: (b, 0, p)),
                        seq(rows, HEAD_DIM)],
            out_specs=seq(rows, HEAD_DIM),
            scratch_shapes=[pltpu.VMEM((rows, 1), _F32), pltpu.VMEM((rows, 1), _F32),
                            pltpu.VMEM((rows, HEAD_DIM), _F32)]),
        out_shape=jax.ShapeDtypeStruct(qs.shape, _BF16),
        compiler_params=_params("parallel", "arbitrary"),
        name="dsa_sample",
    )(page_table, qs, kn, vn, *([cache_k] * pp), *([cache_v] * pp), bias_new, bias_cache, zs)


def _rope_tables(pos, reps):
    def base(half):
        inv_freq = jnp.power(jnp.float32(ROPE_THETA), -jnp.arange(half, dtype=_F32) / half)
        ang = pos.astype(_F32)[:, None] * inv_freq[None, :]
        return jnp.cos(ang), jnp.sin(ang)

    tile = lambda a: jnp.tile(a, (reps, 1))
    c, s = base(HEAD_DIM // 2)
    head = (tile(jnp.concatenate([c, c], axis=1)), tile(jnp.concatenate([-s, s], axis=1)))
    c, s = base(IDX_DIM // 2)
    z = jnp.zeros_like(s)
    c64, lo64, hi64 = (jnp.concatenate([c, c], axis=1), jnp.concatenate([-s, z], axis=1),
                       jnp.concatenate([z, s], axis=1))
    idx = tuple(tile(jnp.concatenate([a, a], axis=1)) for a in (c64, lo64, hi64))
    one, zero = jnp.ones_like(c64), jnp.zeros_like(c64)
    kiwi = (tile(jnp.concatenate([c64, one], axis=1)), tile(jnp.concatenate([lo64, zero], axis=1)),
            tile(jnp.concatenate([hi64, zero], axis=1)))
    return head, idx, kiwi


_ROPE128 = (LANES // 2,)
_ROPE64 = (LANES - IDX_DIM // 2, IDX_DIM // 2)

_OFF = dict(q_sb=0, k_sb=768, v_sb=1536, z_sb=2304, q_d=3072, k_d=3840, v_d=4608, z_d=5376, qi=6144)
_KIWI0 = 7168
_TAIL0 = _KIWI0 + IDX_DIM + IDX_HEADS


def _project_all(h, w_main, w_kiwi, w_tail, layer, tabs, d_model, depth, st, kv_dtype, seq_len):
    head_t, idx_t, kiwi_t = tabs
    pj = functools.partial(_proj, h)
    kv = lambda name: [("heads_major", st[name], depth, seq_len), ("flat", kv_dtype)]
    o = {}
    (o["q_sb"],) = pj(w_main, layer, _OFF["q_sb"], SB_WIDTH, [("flat", _BF16)], name="proj_q_sb")
    st["k_sb"], o["k_sb"] = pj(w_main, layer, _OFF["k_sb"], SB_WIDTH, kv("k_sb"), name="proj_k_sb")
    st["v_sb"], o["v_sb"] = pj(w_main, layer, _OFF["v_sb"], SB_WIDTH, kv("v_sb"), name="proj_v_sb")
    (o["z_sb"],) = pj(w_main, layer, _OFF["z_sb"], SB_WIDTH, [("flat", _F32)], name="proj_z_sb")
    (o["q_d"],) = pj(w_main, layer, _OFF["q_d"], DSA_WIDTH, [("flat", _BF16)], rope=(_ROPE128, head_t),
                     name="proj_q_d")
    st["k_d"], o["k_d"] = pj(w_main, layer, _OFF["k_d"], DSA_WIDTH, kv("k_d"), rope=(_ROPE128, head_t),
                             name="proj_k_d")
    st["v_d"], o["v_d"] = pj(w_main, layer, _OFF["v_d"], DSA_WIDTH, kv("v_d"), name="proj_v_d")
    (o["z_d"],) = pj(w_main, layer, _OFF["z_d"], DSA_WIDTH, [("flat", _F32)], name="proj_z_d")
    (o["qi"],) = pj(w_main, layer, _OFF["qi"], IDX_WIDTH, [("flat", _BF16)], rope=(_ROPE64, idx_t), name="proj_qi")
    o["kiwi"], o["ki_b"], st["ki"] = pj(
        w_kiwi, layer, 0, LANES,
        [("flat", _F32), ("lanes", IDX_DIM, _BF16), ("lanes_stacked", IDX_DIM, st["ki"], depth)],
        rope=(_ROPE64, kiwi_t), name="proj_kiwi")
    (o["q_m"],) = pj(w_tail, layer, 0, MEM_WIDTH, [("flat", _F32)], name="proj_q_m")
    (o["z_m"],) = pj(w_tail, layer, MEM_WIDTH, MEM_WIDTH, [("flat", _F32)], name="proj_z_m")
    (o["gates"],) = pj(w_tail, layer, 2 * MEM_WIDTH, 3 * d_model, [("flat", _F32)], name="proj_gates")
    return o


def kernel(x_prompt, x_sample, cache_sb_k, cache_sb_v, cache_dsa_k, cache_dsa_v, cache_idx_k, cache_mem_k,
           cache_mem_v, page_table, mem_prompt, norm_g, w_in, w_up_sb, w_up_dsa, w_up_mem, w_out, w_mem_k,
           w_mem_v, final_norm_g):
    b, t, d = x_prompt.shape
    db, ds, _ = x_sample.shape
    depth = w_in.shape[0]
    n_pool = cache_sb_k.shape[1]
    n_pages = page_table.shape[1]
    n_past = n_pages * PAGE_SIZE
    mem_len = mem_prompt.shape[1]
    topk_p = min(TOPK_MAX, t // 4)
    topk_s = min(TOPK_MAX, (n_past + ds) // 4)
    assert t % Q_BLOCK == 0 and w_in.shape[2] == _TAIL0 + 2 * MEM_WIDTH + 3 * d

    w_main = w_in[:, :, :_KIWI0].astype(_BF16)
    w_kiwi = jnp.pad(w_in[:, :, _KIWI0:_TAIL0], ((0, 0), (0, 0), (0, LANES - (_TAIL0 - _KIWI0)))).astype(_BF16)
    w_tail = w_in[:, :, _TAIL0:].astype(_BF16)
    w_up_sb_b, w_up_dsa_b, w_up_mem_b = (w.astype(_BF16) for w in (w_up_sb, w_up_dsa, w_up_mem))
    w_out_b = w_out.astype(_BF16)
    w_mem_k_b, w_mem_v_b = w_mem_k.astype(_BF16), w_mem_v.astype(_BF16)
    gains = jnp.concatenate([norm_g, final_norm_g[None]], axis=0)

    tabs_p = _rope_tables(jnp.arange(t), b)
    tabs_s = _rope_tables(n_past + jnp.arange(ds), db)

    n = b * t
    x = x_prompt.reshape(n, d)
    h = _rmsnorm(x, gains[0], _BF16)
    mem_b = mem_prompt.reshape(b * mem_len, d).astype(_BF16)
    st_p = dict.fromkeys(("k_sb", "v_sb", "k_d", "v_d", "ki", "mk", "mv"))
    for l in range(depth):
        o = _project_all(h, w_main, w_kiwi, w_tail, l, tabs_p, d, depth, st_p, _BF16, t)
        g_sb = _sb_prompt(o["q_sb"], o["k_sb"], o["v_sb"], o["z_sb"], b, t)
        bias = _idx_prompt(o["qi"], o["kiwi"], o["ki_b"], b, t, topk_p)
        g_d = _dsa_prompt(o["q_d"], o["k_d"], o["v_d"], bias, o["z_d"], b, t)
        st_p["mk"], mk = _proj(mem_b, w_mem_k_b, l, 0, MEM_WIDTH, [("heads", st_p["mk"], depth), ("flat", _BF16)],
                               name="proj_mem_k")
        st_p["mv"], mv = _proj(mem_b, w_mem_v_b, l, 0, MEM_WIDTH, [("heads", st_p["mv"], depth), ("flat", _BF16)],
                               name="proj_mem_v")
        g_m = _mem_attend(o["q_m"].reshape(b, t, MEM_WIDTH), mk.reshape(b, mem_len, MEM_WIDTH),
                          mv.reshape(b, mem_len, MEM_WIDTH), o["z_m"].reshape(b, t, MEM_WIDTH))
        mixed = _mix(g_sb, g_d, g_m.reshape(n, MEM_WIDTH), o["gates"], w_up_sb_b, w_up_dsa_b, w_up_mem_b, l)
        last = l == depth - 1
        x, h = _out_proj(x, mixed, w_out_b, l, gains[l + 1], _F32 if last else _BF16)
    y_prompt = h.reshape(b, t, d)

    ns = db * ds
    x = x_sample.reshape(ns, d)
    h = _rmsnorm(x, gains[0], _BF16)

    def stack(a, heads):
        c = a.shape[1] // heads
        return a.reshape(db, ds, heads, c).transpose(0, 2, 1, 3).reshape(db, heads * ds, c)

    def unstack(a, heads):
        c = a.shape[2]
        return a.reshape(db, heads, ds, c).transpose(0, 2, 1, 3).reshape(db * ds, heads * c)

    by_head = lambda c: jnp.transpose(c, (0, 1, 3, 2, 4))
    c_sb_k, c_sb_v, c_d_k, c_d_v = (by_head(c) for c in (cache_sb_k, cache_sb_v, cache_dsa_k, cache_dsa_v))
    c_idx = jnp.transpose(cache_idx_k, (0, 1, 3, 2))

    st_s = dict.fromkeys(("k_sb", "v_sb", "k_d", "v_d", "ki"))
    for l in range(depth):
        o = _project_all(h, w_main, w_kiwi, w_tail, l, tabs_s, d, depth, st_s, _F32, ds)
        ki = o["kiwi"][:, :IDX_DIM]
        wi = o["kiwi"][:, IDX_DIM:IDX_DIM + IDX_HEADS]
        g_sb = _sb_sample(stack(o["q_sb"], SB_HEADS), o["k_sb"].reshape(db, ds, SB_WIDTH),
                          o["v_sb"].reshape(db, ds, SB_WIDTH), c_sb_k, c_sb_v,
                          stack(o["z_sb"], SB_HEADS), page_table, l)
        keys_c, keys_n = _idx_sample(stack(o["qi"], IDX_HEADS), stack(wi, IDX_HEADS),
                                     ki.reshape(db, ds, IDX_DIM), c_idx, page_table, l)
        bias_c, bias_n = _select_sample(keys_c.reshape(ns, n_past), keys_n.reshape(ns, PAGE_SIZE), topk_s)
        g_d = _dsa_sample(stack(o["q_d"], DSA_HEADS), o["k_d"].reshape(db, ds, DSA_WIDTH),
                          o["v_d"].reshape(db, ds, DSA_WIDTH), c_d_k, c_d_v,
                          bias_c.reshape(db, ds, n_past), bias_n.reshape(db, ds, PAGE_SIZE),
                          stack(o["z_d"], DSA_HEADS), page_table, l)
        g_m = _mem_attend(o["q_m"].reshape(db, ds, MEM_WIDTH), cache_mem_k, cache_mem_v,
                          o["z_m"].reshape(db, ds, MEM_WIDTH), layer=l)
        mixed = _mix(unstack(g_sb, SB_HEADS), unstack(g_d, DSA_HEADS), g_m.reshape(ns, MEM_WIDTH), o["gates"],
                     w_up_sb_b, w_up_dsa_b, w_up_mem_b, l)
        last = l == depth - 1
        x, h = _out_proj(x, mixed, w_out_b, l, gains[l + 1], _F32 if last else _BF16)
    y_sample = h.reshape(db, ds, d)

    kv = lambda a: jnp.transpose(a, (0, 1, 3, 2, 4))
    hp = lambda a: a.reshape((depth, b, -1) + a.shape[2:])
    hs = lambda a: a.reshape((depth, db, ds) + a.shape[2:])
    return (y_prompt, y_sample,
            kv(st_p["k_sb"]), kv(st_p["v_sb"]), kv(st_p["k_d"]), kv(st_p["v_d"]), hp(st_p["ki"]),
            hp(st_p["mk"]), hp(st_p["mv"]),
            kv(st_s["k_sb"]), kv(st_s["v_sb"]), kv(st_s["k_d"]), kv(st_s["v_d"]), hs(st_s["ki"]))
```

```python
import functools
import math

import numpy as np
import jax
import jax.numpy as jnp
from jax import lax
from jax.experimental import pallas as pl
from jax.experimental.pallas import tpu as pltpu

HEAD_DIM = 128
SB_HEADS = 6
DSA_HEADS = 6
MEM_HEADS = 4
SB_WIDTH = SB_HEADS * HEAD_DIM
DSA_WIDTH = DSA_HEADS * HEAD_DIM
MEM_WIDTH = MEM_HEADS * HEAD_DIM
IDX_HEADS = 16
IDX_DIM = 64
IDX_WIDTH = IDX_HEADS * IDX_DIM
PAGE_SIZE = 128
TOPK_MAX = 256
Q_BLOCK = 128
ROPE_THETA = 10000.0
NORM_EPS = 1e-6
ATTN_SCALE = HEAD_DIM ** -0.5
LOG2E = math.log2(math.e)
ATTN_SCALE2 = ATTN_SCALE * LOG2E
IDX_SCALE = (IDX_HEADS * IDX_DIM) ** -0.5

SB_DEAD_LOG2 = -160.0
LANES = 128
NEG_BIAS = -1e30
INT32_MIN = -2 ** 31
KEY_NEG_INF = -2139095041
VMEM_LIMIT = 56 * 2 ** 20

_F32 = jnp.float32
_BF16 = jnp.bfloat16
_I32 = jnp.int32
_NT = (((1,), (1,)), ((), ()))


def _params(*sem):
    return pltpu.CompilerParams(dimension_semantics=sem, vmem_limit_bytes=VMEM_LIMIT)


def _sigmoid(x):
    return 1.0 / (1.0 + jnp.exp(-x))


def _silu(x):
    return x * _sigmoid(x)


def _softplus2(t):
    return jnp.maximum(t, 0.0) + jnp.log2(1.0 + jnp.exp2(-jnp.abs(t)))


def _split_bf16(x):
    hi = x.astype(_BF16)
    lo = (x - hi.astype(_F32)).astype(_BF16)
    return hi, lo


def _sortable_key(s):
    b = lax.bitcast_convert_type(s, _I32)
    return b ^ ((b >> 31) & 0x7FFFFFFF)


def _largest_divisor(n, candidates):
    for c in candidates:
        if n % c == 0:
            return c
    raise ValueError(f"no tile in {candidates} divides {n}")


def _rms_kernel(x_ref, g_ref, o_ref):
    x = x_ref[...]
    y = x * lax.rsqrt(jnp.mean(x * x, axis=-1, keepdims=True) + NORM_EPS)
    o_ref[...] = (y * g_ref[...]).astype(o_ref.dtype)


def _rmsnorm(x, g, out_dtype):
    n, d = x.shape
    tm = _largest_divisor(n, (512, 256, 128, 64))
    return pl.pallas_call(
        _rms_kernel,
        grid=(n // tm,),
        in_specs=[pl.BlockSpec((tm, d), lambda i: (i, 0)),
                  pl.BlockSpec((1, d), lambda i: (0, 0))],
        out_specs=pl.BlockSpec((tm, d), lambda i: (i, 0)),
        out_shape=jax.ShapeDtypeStruct((n, d), out_dtype),
        compiler_params=_params("parallel"),
        name="rmsnorm",
    )(x, g.reshape(1, d))


def _proj_kernel(*refs, shifts, kinds, n_alias):
    h_ref, w_ref = refs[0], refs[1]
    n_tab = 1 + len(shifts) if shifts else 0
    tabs = refs[2:2 + n_tab]
    outs = refs[2 + n_tab + n_alias:]
    z = jnp.dot(h_ref[...], w_ref[...], preferred_element_type=_F32)
    tn = z.shape[1]
    for g in range(tn // LANES):
        zg = z[:, g * LANES:(g + 1) * LANES]
        if shifts:
            r = zg * tabs[0][...]
            for t, sh in enumerate(shifts):
                r = r + pltpu.roll(zg, sh, 1) * tabs[1 + t][...]
            zg = r
        for kind, o_ref in zip(kinds, outs):
            if kind == "flat":
                o_ref[:, g * LANES:(g + 1) * LANES] = zg.astype(o_ref.dtype)
            elif kind == "heads":
                o_ref[:, g, :] = zg
            elif kind == "heads_major":
                if len(o_ref.shape) == 3:
                    o_ref[g] = zg
                else:
                    gr = o_ref.shape[2]
                    for b in range(o_ref.shape[0]):
                        o_ref[b, g] = zg[b * gr:(b + 1) * gr]
            else:
                o_ref[...] = zg[:, :o_ref.shape[1]].astype(o_ref.dtype)


def _proj(h, w, layer, col0, width, outs, rope=None, name="proj"):
    n, d = h.shape
    tm = _largest_divisor(n, (1024, 512, 256, 128, 64))
    tn = _largest_divisor(width, (1024, 768, 512, 256, 128))
    assert col0 % tn == 0
    cb = col0 // tn
    shifts, tables = rope if rope else ((), ())
    in_specs = [pl.BlockSpec((tm, d), lambda j, i: (i, 0)),
                pl.BlockSpec((None, d, tn), lambda j, i: (layer, 0, cb + j))]
    operands = [h, w]
    for t in tables:
        assert t.shape == (n, LANES)
        in_specs.append(pl.BlockSpec((tm, LANES), lambda j, i: (i, 0)))
        operands.append(t)
    kinds, out_specs, out_shape, aliases = [], [], [], {}
    prevs = []
    for req in outs:
        kind = req[0]
        if kind == "flat":
            out_specs.append(pl.BlockSpec((tm, tn), lambda j, i: (i, j)))
            out_shape.append(jax.ShapeDtypeStruct((n, width), req[1]))
        elif kind == "lanes":
            assert width == LANES
            out_specs.append(pl.BlockSpec((tm, req[1]), lambda j, i: (i, 0)))
            out_shape.append(jax.ShapeDtypeStruct((n, req[1]), req[2]))
        elif kind == "heads":
            assert tn == width
            nh = width // LANES
            out_specs.append(pl.BlockSpec((None, tm, nh, LANES), lambda j, i: (layer, i, 0, 0)))
            out_shape.append(jax.ShapeDtypeStruct((req[2], n, nh, LANES), _F32))
            prevs.append((len(kinds), req[1]))
        elif kind == "heads_major":
            assert tn == width
            nh = width // LANES
            gr = req[3]
            if tm <= gr:
                per = gr // tm
                out_specs.append(pl.BlockSpec((None, None, nh, tm, LANES),
                                              lambda j, i, per=per: (layer, i // per, 0, i % per, 0)))
            else:
                out_specs.append(pl.BlockSpec((None, tm // gr, nh, gr, LANES), lambda j, i: (layer, i, 0, 0, 0)))
            out_shape.append(jax.ShapeDtypeStruct((req[2], n // gr, nh, gr, LANES), _F32))
            prevs.append((len(kinds), req[1]))
        elif kind == "lanes_stacked":
            assert width == LANES
            out_specs.append(pl.BlockSpec((None, tm, req[1]), lambda j, i: (layer, i, 0)))
            out_shape.append(jax.ShapeDtypeStruct((req[3], n, req[1]), _F32))
            prevs.append((len(kinds), req[2]))
        else:
            raise ValueError(kind)
        kinds.append(kind)
    n_alias = 0
    for out_idx, prev in prevs:
        if prev is not None:
            aliases[len(operands)] = out_idx
            in_specs.append(pl.BlockSpec(memory_space=pl.ANY))
            operands.append(prev)
            n_alias += 1
    return pl.pallas_call(
        functools.partial(_proj_kernel, shifts=tuple(shifts), kinds=tuple(kinds), n_alias=n_alias),
        grid=(width // tn, n // tm),
        in_specs=in_specs,
        out_specs=out_specs,
        out_shape=out_shape,
        input_output_aliases=aliases,
        compiler_params=_params("parallel", "parallel"),
        name=name,
    )(*operands)


def _cumsum_matrix(t):
    r = np.arange(2 * t) % t
    c = np.arange(2 * t)
    m = np.where(c[None, :] < t, r[:, None] > c[None, :], True)
    return jnp.asarray(m, dtype=_BF16)


def _sb_prompt_kernel(q_ref, k_ref, v_ref, z_ref, u_ref, o_ref, carry_ref, acc_ref, *, tq):
    i = pl.program_id(1)
    heads = SB_HEADS
    hsl = [slice(h * HEAD_DIM, (h + 1) * HEAD_DIM) for h in range(heads)]
    rsl = [slice(h * tq, (h + 1) * tq) for h in range(heads)]

    def key_tiles(off, nt, before):
        n = heads * tq
        s = jnp.concatenate(
            [lax.dot_general(q_ref[:, hsl[h]], k_ref[pl.ds(off + (nt - 1 - j) * tq, tq), hsl[h]], _NT,
                             preferred_element_type=_F32)
             for j in range(nt) for h in range(heads)], axis=0) * ATTN_SCALE2
        sp = _softplus2(s)
        lk = -sp
        if before is not None:
            lk = jnp.where(before, lk, 0.0)
        hi, lo = _split_bf16(lk)
        cr = jnp.dot(jnp.concatenate([hi, lo], axis=1), u_ref[...], preferred_element_type=_F32)
        carry = carry_ref[...]
        for j in range(nt):
            r = slice(j * n, (j + 1) * n)
            a = jnp.exp2(s[r] - sp[r] + carry + cr[r, :tq])
            if before is not None:
                a = jnp.where(before, a, 0.0)
            carry = carry + cr[r, tq:]
            ab = a.astype(_BF16)
            for h in range(heads):
                acc_ref[rsl[h], :] += jnp.dot(ab[rsl[h]], v_ref[pl.ds(off + (nt - 1 - j) * tq, tq), hsl[h]],
                                              preferred_element_type=_F32)
        carry_ref[...] = carry

    carry_ref[...] = jnp.zeros_like(carry_ref)
    acc_ref[...] = jnp.zeros_like(acc_ref)
    row = lax.broadcasted_iota(_I32, (heads * tq, tq), 0) & (tq - 1)
    col = lax.broadcasted_iota(_I32, (heads * tq, tq), 1)
    key_tiles(pl.multiple_of(i * tq, tq), 1, col < row)

    def live():
        return jnp.max(carry_ref[...]) > SB_DEAD_LOG2

    def body(c):
        key_tiles(pl.multiple_of((i - 2 - 2 * c[0]) * tq, tq), 2, None)
        return c[0] + 1, live()

    _, alive = lax.while_loop(lambda c: jnp.logical_and(c[0] < i // 2, c[1]), body, (jnp.int32(0), live()))

    @pl.when(jnp.logical_and(i % 2 == 1, alive))
    def _():
        key_tiles(0, 1, None)
    for h in range(heads):
        o_ref[:, hsl[h]] = (acc_ref[rsl[h], :] * _silu(z_ref[:, hsl[h]])).astype(o_ref.dtype)


def _sb_prompt(q, k, v, z, b, t):
    tq = Q_BLOCK
    nq = t // tq
    w = SB_WIDTH
    u = _cumsum_matrix(tq)
    return pl.pallas_call(
        functools.partial(_sb_prompt_kernel, tq=tq),
        grid=(b, nq),
        in_specs=[pl.BlockSpec((tq, w), lambda bi, i: (bi * nq + i, 0)),
                  pl.BlockSpec((t, w), lambda bi, i: (bi, 0)),
                  pl.BlockSpec((t, w), lambda bi, i: (bi, 0)),
                  pl.BlockSpec((tq, w), lambda bi, i: (bi * nq + i, 0)),
                  pl.BlockSpec((2 * tq, 2 * tq), lambda bi, i: (0, 0))],
        out_specs=pl.BlockSpec((tq, w), lambda bi, i: (bi * nq + i, 0)),
        out_shape=jax.ShapeDtypeStruct(q.shape, _BF16),
        scratch_shapes=[pltpu.VMEM((SB_HEADS * tq, tq), _F32), pltpu.VMEM((SB_HEADS * tq, HEAD_DIM), _F32)],
        compiler_params=_params("parallel", "arbitrary"),
        name="sb_prompt",
    )(q, k, v, z, u)


def _select_to_bias(key_ref, bias_ref, nch, ch, kk):
    rows = key_ref.shape[0]

    def count(pred):
        def body(c, part):
            off = pl.multiple_of(c * ch, ch)
            m = jnp.where(pred(key_ref[:, pl.ds(off, ch)], off), 1, 0).astype(_I32)
            for g in range(ch // LANES):
                part = part + m[:, g * LANES:(g + 1) * LANES]
            return part
        part = lax.fori_loop(0, nch, body, jnp.zeros((rows, LANES), _I32))
        return jnp.sum(part, axis=1, keepdims=True)

    def bit_body(it, u):
        cand_u = u | lax.shift_left(jnp.int32(1), 31 - it)
        cand = cand_u ^ jnp.int32(INT32_MIN)
        cnt = count(lambda kc, off: kc >= cand)
        return jnp.where(cnt >= kk, cand_u, u)

    u = lax.fori_loop(0, 32, bit_body, jnp.zeros((rows, 1), _I32))
    thr = u ^ jnp.int32(INT32_MIN)
    cnt_gt = count(lambda kc, off: kc > thr)
    cnt_ge = count(lambda kc, off: kc >= thr)
    need = kk - cnt_gt
    excess = jnp.where((cnt_ge > kk) & (thr > KEY_NEG_INF), 1, 0).astype(_I32)
    nbits = max(1, int(math.ceil(math.log2(key_ref.shape[1] + 1))))

    def tie_cut(_):
        def q_body(it, q):
            cand = q | lax.shift_left(jnp.int32(1), nbits - 1 - it)

            def pred(kc, off):
                idx = off + lax.broadcasted_iota(_I32, kc.shape, 1)
                return (kc == thr) & (idx < cand)
            return jnp.where(count(pred) < need, cand, q)
        return lax.fori_loop(0, nbits, q_body, jnp.zeros((rows, 1), _I32))

    def no_cut(_):
        return jnp.full((rows, 1), 2 ** 30, _I32)

    qcut = lax.cond(jnp.max(excess) > 0, tie_cut, no_cut, 0)

    def write(c, _):
        off = pl.multiple_of(c * ch, ch)
        kc = key_ref[:, pl.ds(off, ch)]
        idx = off + lax.broadcasted_iota(_I32, kc.shape, 1)
        sel = (kc > thr) | ((kc == thr) & (idx <= qcut))
        ok = sel & (kc > KEY_NEG_INF)
        bias_ref[:, pl.ds(off, ch)] = jnp.where(ok, 0.0, NEG_BIAS).astype(_F32)
        return 0

    lax.fori_loop(0, nch, write, 0)


def _idx_prompt_kernel(qi_ref, wi_ref, ki_ref, bias_ref, key_ref, *, tq, ch, kk):
    i = pl.program_id(1)
    nch = (i * tq + tq + ch - 1) // ch
    qi = qi_ref[...]
    wi = wi_ref[...]
    rowpos = i * tq + lax.broadcasted_iota(_I32, (tq, ch), 0)

    def score(c, _):
        off = pl.multiple_of(c * ch, ch)
        kc = ki_ref[pl.ds(off, ch), :]
        acc = jnp.zeros((tq, ch), _F32)
        for h in range(IDX_HEADS):
            r = lax.dot_general(qi[:, h * IDX_DIM:(h + 1) * IDX_DIM], kc, _NT,
                                preferred_element_type=_F32)
            acc = acc + jnp.maximum(r, 0.0) * wi[:, IDX_DIM + h:IDX_DIM + h + 1]
        sc = acc * IDX_SCALE
        sc = jnp.where(sc == 0.0, 0.0, sc)
        colpos = off + lax.broadcasted_iota(_I32, (tq, ch), 1)
        sc = jnp.where(colpos <= rowpos, sc, -jnp.inf)
        key_ref[:, pl.ds(off, ch)] = _sortable_key(sc)
        return 0

    lax.fori_loop(0, nch, score, 0)
    bias_ref[...] = jnp.full(bias_ref.shape, NEG_BIAS, _F32)
    _select_to_bias(key_ref, bias_ref, nch, ch, kk)


def _idx_prompt(qi, kiwi, ki, b, t, kk):
    tq = Q_BLOCK
    nq = t // tq
    ch = _largest_divisor(t, (512, 256, 128))
    return pl.pallas_call(
        functools.partial(_idx_prompt_kernel, tq=tq, ch=ch, kk=kk),
        grid=(b, nq),
        in_specs=[pl.BlockSpec((tq, IDX_WIDTH), lambda bi, i: (bi * nq + i, 0)),
                  pl.BlockSpec((tq, LANES), lambda bi, i: (bi * nq + i, 0)),
                  pl.BlockSpec((t, IDX_DIM), lambda bi, i: (bi, 0))],
        out_specs=pl.BlockSpec((tq, t), lambda bi, i: (bi * nq + i, 0)),
        out_shape=jax.ShapeDtypeStruct((b * t, t), _F32),
        scratch_shapes=[pltpu.VMEM((tq, t), _I32)],
        compiler_params=_params("parallel", "arbitrary"),
        name="idx_select_prompt",
    )(qi, kiwi, ki)


def _dsa_prompt_kernel(q_ref, k_ref, v_ref, bias_ref, z_ref, o_ref, m_ref, l_ref, acc_ref, *, tq, ch):
    i = pl.program_id(1)
    nch = (i * tq + tq + ch - 1) // ch
    heads = DSA_HEADS
    hsl = [slice(h * HEAD_DIM, (h + 1) * HEAD_DIM) for h in range(heads)]
    rsl = [slice(h * tq, (h + 1) * tq) for h in range(heads)]
    m_ref[...] = jnp.full(m_ref.shape, NEG_BIAS, _F32)
    l_ref[...] = jnp.zeros_like(l_ref)
    acc_ref[...] = jnp.zeros_like(acc_ref)
    ones = jnp.ones((ch, LANES), _BF16)

    def body(c, _):
        off = pl.multiple_of(c * ch, ch)
        bias = bias_ref[:, pl.ds(off, ch)]
        s = jnp.concatenate(
            [lax.dot_general(q_ref[:, hsl[h]], k_ref[pl.ds(off, ch), hsl[h]], _NT, preferred_element_type=_F32)
             for h in range(heads)], axis=0) * (ATTN_SCALE * LOG2E) + jnp.concatenate([bias] * heads, axis=0)
        m_old = m_ref[...]
        m_new = jnp.maximum(m_old, jnp.max(s, axis=1, keepdims=True))
        alpha = jnp.exp2(m_old - m_new)
        pb = jnp.exp2(s - m_new).astype(_BF16)
        l_ref[...] = l_ref[...] * alpha + jnp.dot(pb, ones, preferred_element_type=_F32)
        for h in range(heads):
            acc_ref[rsl[h], :] = acc_ref[rsl[h], :] * alpha[rsl[h]] + jnp.dot(
                pb[rsl[h]], v_ref[pl.ds(off, ch), hsl[h]], preferred_element_type=_F32)
        m_ref[...] = m_new
        return 0

    lax.fori_loop(0, nch, body, 0)
    for h in range(heads):
        o_ref[:, hsl[h]] = (acc_ref[rsl[h], :] / l_ref[rsl[h], :] * _silu(z_ref[:, hsl[h]])).astype(o_ref.dtype)


def _dsa_prompt(q, k, v, bias, z, b, t):
    tq = Q_BLOCK
    nq = t // tq
    ch = _largest_divisor(t, (1024, 512, 256, 128))
    w = DSA_WIDTH
    return pl.pallas_call(
        functools.partial(_dsa_prompt_kernel, tq=tq, ch=ch),
        grid=(b, nq),
        in_specs=[pl.BlockSpec((tq, w), lambda bi, i: (bi * nq + i, 0)),
                  pl.BlockSpec((t, w), lambda bi, i: (bi, 0)),
                  pl.BlockSpec((t, w), lambda bi, i: (bi, 0)),
                  pl.BlockSpec((tq, t), lambda bi, i: (bi * nq + i, 0)),
                  pl.BlockSpec((tq, w), lambda bi, i: (bi * nq + i, 0))],
        out_specs=pl.BlockSpec((tq, w), lambda bi, i: (bi * nq + i, 0)),
        out_shape=jax.ShapeDtypeStruct(q.shape, _BF16),
        scratch_shapes=[pltpu.VMEM((DSA_HEADS * tq, 1), _F32), pltpu.VMEM((DSA_HEADS * tq, LANES), _F32),
                        pltpu.VMEM((DSA_HEADS * tq, HEAD_DIM), _F32)],
        compiler_params=_params("parallel", "arbitrary"),
        name="dsa_prompt",
    )(q, k, v, bias, z)


def _mem_kernel(q_ref, mk_ref, mv_ref, z_ref, o_ref):
    tm = q_ref.shape[0]
    pad = (-tm) % 16
    by_head = len(mk_ref.shape) == 3
    for h in range(MEM_HEADS):
        hs = slice(h * HEAD_DIM, (h + 1) * HEAD_DIM)
        mk = mk_ref[:, h, :] if by_head else mk_ref[:, hs]
        mv = mv_ref[:, h, :] if by_head else mv_ref[:, hs]
        qh = q_ref[:, hs].astype(_F32)
        if pad:
            qh = jnp.concatenate([qh, jnp.zeros((pad, HEAD_DIM), _F32)], axis=0)
        s = lax.dot_general(qh.astype(_BF16), mk.astype(_BF16), _NT, preferred_element_type=_F32) * ATTN_SCALE2
        p = jnp.exp2(s - jnp.max(s, axis=1, keepdims=True))
        p = p / jnp.sum(p, axis=1, keepdims=True)
        o = jnp.dot(p.astype(_BF16), mv.astype(_BF16), preferred_element_type=_F32)
        o_ref[:, hs] = (o[:tm] * _silu(z_ref[:, hs])).astype(o_ref.dtype)


def _mem_attend(q, mk, mv, z, layer=None):
    g, r, w = q.shape
    tm = _largest_divisor(r, (512, 256, 128, 64, 8))
    if layer is None:
        m = mk.shape[1]
        mem = pl.BlockSpec((None, m, w), lambda gi, i: (gi, 0, 0))
    else:
        m = mk.shape[2]
        mem = pl.BlockSpec((None, None, m, MEM_HEADS, HEAD_DIM), lambda gi, i: (layer, gi, 0, 0, 0))
    return pl.pallas_call(
        _mem_kernel,
        grid=(g, r // tm),
        in_specs=[pl.BlockSpec((None, tm, w), lambda gi, i: (gi, i, 0)),
                  mem, mem,
                  pl.BlockSpec((None, tm, w), lambda gi, i: (gi, i, 0))],
        out_specs=pl.BlockSpec((None, tm, w), lambda gi, i: (gi, i, 0)),
        out_shape=jax.ShapeDtypeStruct((g, r, w), _BF16),
        compiler_params=_params("parallel", "parallel"),
        name="mem_attend",
    )(q, mk, mv, z)


def _mix_kernel(gs_ref, gd_ref, gm_ref, g1_ref, g2_ref, g3_ref, w1_ref, w2_ref, w3_ref, o_ref):
    a = jnp.dot(gs_ref[...], w1_ref[...], preferred_element_type=_F32)
    c = jnp.dot(gd_ref[...], w2_ref[...], preferred_element_type=_F32)
    m = jnp.dot(gm_ref[...], w3_ref[...], preferred_element_type=_F32)
    mixed = _sigmoid(g1_ref[...]) * a + _sigmoid(g2_ref[...]) * c + _sigmoid(g3_ref[...]) * m
    o_ref[...] = mixed.astype(o_ref.dtype)


def _mix(gs, gd, gm, gates, w1, w2, w3, layer):
    n = gs.shape[0]
    d = w1.shape[2]
    tm = _largest_divisor(n, (256, 128, 64))
    row = lambda w: pl.BlockSpec((tm, w), lambda i: (i, 0))
    wspec = lambda w: pl.BlockSpec((None, w.shape[1], d), lambda i: (layer, 0, 0))
    return pl.pallas_call(
        _mix_kernel,
        grid=(n // tm,),
        in_specs=[row(SB_WIDTH), row(DSA_WIDTH), row(MEM_WIDTH),
                  pl.BlockSpec((tm, d), lambda i: (i, 0)),
                  pl.BlockSpec((tm, d), lambda i: (i, 1)),
                  pl.BlockSpec((tm, d), lambda i: (i, 2)),
                  wspec(w1), wspec(w2), wspec(w3)],
        out_specs=pl.BlockSpec((tm, d), lambda i: (i, 0)),
        out_shape=jax.ShapeDtypeStruct((n, d), _BF16),
        compiler_params=_params("parallel"),
        name="mix",
    )(gs, gd, gm, gates, gates, gates, w1, w2, w3)


def _out_kernel(x_ref, m_ref, w_ref, g_ref, xo_ref, ho_ref):
    xn = x_ref[...] + jnp.dot(m_ref[...], w_ref[...], preferred_element_type=_F32)
    xo_ref[...] = xn
    y = xn * lax.rsqrt(jnp.mean(xn * xn, axis=-1, keepdims=True) + NORM_EPS)
    ho_ref[...] = (y * g_ref[...]).astype(ho_ref.dtype)


def _out_proj(x, mixed, w, layer, g, h_dtype):
    n, d = x.shape
    tm = _largest_divisor(n, (512, 256, 128, 64))
    return pl.pallas_call(
        _out_kernel,
        grid=(n // tm,),
        in_specs=[pl.BlockSpec((tm, d), lambda i: (i, 0)),
                  pl.BlockSpec((tm, d), lambda i: (i, 0)),
                  pl.BlockSpec((None, d, d), lambda i: (layer, 0, 0)),
                  pl.BlockSpec((1, d), lambda i: (0, 0))],
        out_specs=[pl.BlockSpec((tm, d), lambda i: (i, 0)),
                   pl.BlockSpec((tm, d), lambda i: (i, 0))],
        out_shape=[jax.ShapeDtypeStruct((n, d), _F32), jax.ShapeDtypeStruct((n, d), h_dtype)],
        compiler_params=_params("parallel"),
        name="out_proj",
    )(x, mixed, w, g.reshape(1, d))


def _head_rows(x, ds, h):
    return x[h * ds:(h + 1) * ds]


def _stack_scores(qs, key_heads, ds):
    parts = []
    for h, kh in enumerate(key_heads):
        r = lax.dot_general(qs, kh, _NT, preferred_element_type=_F32)
        parts.append(_head_rows(r, ds, h))
    return jnp.concatenate(parts, axis=0) * ATTN_SCALE2


def _stack_pv(p, val_heads, ds):
    pb = p.astype(_BF16)
    parts = []
    for h, vh in enumerate(val_heads):
        r = jnp.dot(pb, vh, preferred_element_type=_F32)
        parts.append(_head_rows(r, ds, h))
    return jnp.concatenate(parts, axis=0)


def _pad_rows(x, rows):
    return jnp.concatenate([x, jnp.zeros((rows - x.shape[0], x.shape[1]), x.dtype)], axis=0)


def _page_heads(ref):
    return [ref[h].astype(_BF16) for h in range(ref.shape[0])]


def _new_heads(ref, heads):
    x = _pad_rows(ref[...], PAGE_SIZE).astype(_BF16)
    return [x[:, h * HEAD_DIM:(h + 1) * HEAD_DIM] for h in range(heads)]


def _pages_per_step(n_pages, candidates):
    return _largest_divisor(n_pages, candidates)


def _sb_sample_kernel(pt_ref, q_ref, kn_ref, vn_ref, z_ref, u_ref, kc_hbm, vc_hbm, o_ref,
                      kbuf, vbuf, sem, carry_ref, acc_ref, *, ds, n_pages, pp, layer):
    b = pl.program_id(0)
    n_steps = n_pages // pp
    rows = SB_HEADS * ds
    qs = q_ref[...]

    def page_copies(step, slot):
        cps = []
        for k in range(pp):
            page = pt_ref[b, n_pages - 1 - (step * pp + k)]
            cps.append(pltpu.make_async_copy(kc_hbm.at[layer, page], kbuf.at[slot, k], sem.at[slot, 0, k]))
            cps.append(pltpu.make_async_copy(vc_hbm.at[layer, page], vbuf.at[slot, k], sem.at[slot, 1, k]))
        return cps

    def start(step, slot):
        for cp in page_copies(step, slot):
            cp.start()

    def wait(step, slot):
        for cp in page_copies(step, slot):
            cp.wait()

    start(0, 0)

    def page(key_heads, val_heads, before, carry, acc):
        s = _stack_scores(qs, key_heads, ds)
        sp = _softplus2(s)
        lk = -sp
        if before is not None:
            lk = jnp.where(before, lk, 0.0)
        hi, lo = _split_bf16(lk)
        cr = jnp.dot(jnp.concatenate([hi, lo], axis=1), u_ref[...], preferred_element_type=_F32)
        a = jnp.exp2(s - sp + carry + cr[:, :PAGE_SIZE])
        if before is not None:
            a = jnp.where(before, a, 0.0)
        return carry + cr[:, PAGE_SIZE:], acc + _stack_pv(a, val_heads, ds)

    qrow = lax.broadcasted_iota(_I32, (rows, PAGE_SIZE), 0) % ds
    col = lax.broadcasted_iota(_I32, (rows, PAGE_SIZE), 1)
    carry, acc = page(_new_heads(kn_ref, SB_HEADS), _new_heads(vn_ref, SB_HEADS), col < qrow,
                      jnp.zeros((rows, PAGE_SIZE), _F32), jnp.zeros((rows, HEAD_DIM), _F32))
    carry_ref[...] = carry
    acc_ref[...] = acc

    def live():
        return jnp.max(carry_ref[...]) > SB_DEAD_LOG2

    def body(c):
        step = c[0]
        slot = step % 2

        @pl.when(step + 1 < n_steps)
        def _():
            start(step + 1, 1 - slot)

        wait(step, slot)
        carry, acc = carry_ref[...], acc_ref[...]
        s = jnp.concatenate([_stack_scores(qs, _page_heads(kbuf.at[slot, k]), ds) for k in range(pp)], axis=0)
        sp = _softplus2(s)
        hi, lo = _split_bf16(-sp)
        cr = jnp.dot(jnp.concatenate([hi, lo], axis=1), u_ref[...], preferred_element_type=_F32)
        weights = []
        for k in range(pp):
            r = slice(k * rows, (k + 1) * rows)
            weights.append(jnp.exp2(s[r] - sp[r] + carry + cr[r, :PAGE_SIZE]))
            carry = carry + cr[r, PAGE_SIZE:]
        for k in range(pp):
            acc = acc + _stack_pv(weights[k], _page_heads(vbuf.at[slot, k]), ds)
        carry_ref[...] = carry
        acc_ref[...] = acc
        return step + 1, live()

    done, _ = lax.while_loop(lambda c: jnp.logical_and(c[0] < n_steps, c[1]), body, (jnp.int32(0), live()))

    @pl.when(done < n_steps)
    def _():
        wait(done, done % 2)

    o_ref[...] = (acc_ref[...] * _silu(z_ref[...])).astype(o_ref.dtype)


def _sb_sample(qs, kn, vn, cache_k, cache_v, zs, page_table, layer):
    db, rows, _ = qs.shape
    ds = rows // SB_HEADS
    n_pages = page_table.shape[1]
    pp = _pages_per_step(n_pages, (4, 2, 1))
    w = SB_WIDTH
    u = _cumsum_matrix(PAGE_SIZE)
    seq = lambda r, c: pl.BlockSpec((None, r, c), lambda b, pt: (b, 0, 0))
    hbm = pl.BlockSpec(memory_space=pl.ANY)
    page_buf = pltpu.VMEM((2, pp, SB_HEADS, PAGE_SIZE, HEAD_DIM), _F32)
    return pl.pallas_call(
        functools.partial(_sb_sample_kernel, ds=ds, n_pages=n_pages, pp=pp, layer=layer),
        grid_spec=pltpu.PrefetchScalarGridSpec(
            num_scalar_prefetch=1,
            grid=(db,),
            in_specs=[seq(rows, HEAD_DIM), seq(ds, w), seq(ds, w), seq(rows, HEAD_DIM),
                      pl.BlockSpec((2 * PAGE_SIZE, 2 * PAGE_SIZE), lambda b, pt: (0, 0)), hbm, hbm],
            out_specs=seq(rows, HEAD_DIM),
            scratch_shapes=[page_buf, page_buf, pltpu.SemaphoreType.DMA((2, 2, pp)),
                            pltpu.VMEM((rows, PAGE_SIZE), _F32), pltpu.VMEM((rows, HEAD_DIM), _F32)]),
        out_shape=jax.ShapeDtypeStruct(qs.shape, _BF16),
        compiler_params=_params("arbitrary"),
        name="sb_sample",
    )(page_table, qs, kn, vn, zs, u, cache_k, cache_v)


def _idx_sample_kernel(pt_ref, qi_ref, wi_ref, kn_ref, *rest, ds, pp):
    kc = rest[:pp]
    oc_ref, on_ref = rest[pp:]
    p = pl.program_id(1)

    def scores(r):
        r = jnp.maximum(r, 0.0) * wi_ref[...]
        sc = _head_rows(r, ds, 0)
        for h in range(1, IDX_HEADS):
            sc = sc + _head_rows(r, ds, h)
        sc = sc * IDX_SCALE
        return jnp.where(sc == 0.0, 0.0, sc)

    for k in range(pp):
        r = jnp.dot(qi_ref[...], kc[k][...].astype(_BF16), preferred_element_type=_F32)
        oc_ref[:, k * PAGE_SIZE:(k + 1) * PAGE_SIZE] = _sortable_key(scores(r))

    @pl.when(p == 0)
    def _():
        kn = _pad_rows(kn_ref[...], PAGE_SIZE).astype(_BF16)
        sc = scores(lax.dot_general(qi_ref[...], kn, _NT, preferred_element_type=_F32))
        s_row = lax.broadcasted_iota(_I32, (ds, PAGE_SIZE), 0)
        col = lax.broadcasted_iota(_I32, (ds, PAGE_SIZE), 1)
        on_ref[...] = _sortable_key(jnp.where(col <= s_row, sc, -jnp.inf))


def _idx_sample(qi_t, wi_col, ki_new, cache_ik, page_table, layer):
    db, rows, _ = qi_t.shape
    ds = rows // IDX_HEADS
    n_pages = page_table.shape[1]
    pp = _pages_per_step(n_pages, (16, 8, 4, 2, 1))
    seq = lambda r, c: pl.BlockSpec((None, r, c), lambda b, p, pt: (b, 0, 0))
    pages = [pl.BlockSpec((None, None, IDX_DIM, PAGE_SIZE),
                          lambda b, p, pt, k=k: (layer, pt[b, p * pp + k], 0, 0)) for k in range(pp)]
    return pl.pallas_call(
        functools.partial(_idx_sample_kernel, ds=ds, pp=pp),
        grid_spec=pltpu.PrefetchScalarGridSpec(
            num_scalar_prefetch=1,
            grid=(db, n_pages // pp),
            in_specs=[seq(rows, IDX_DIM), seq(rows, 1), seq(ds, IDX_DIM)] + pages,
            out_specs=[pl.BlockSpec((None, ds, pp * PAGE_SIZE), lambda b, p, pt: (b, 0, p)),
                       seq(ds, PAGE_SIZE)]),
        out_shape=[jax.ShapeDtypeStruct((db, ds, n_pages * PAGE_SIZE), _I32),
                   jax.ShapeDtypeStruct((db, ds, PAGE_SIZE), _I32)],
        compiler_params=_params("parallel", "arbitrary"),
        name="idx_scores_sample",
    )(page_table, qi_t, wi_col, ki_new, *([cache_ik] * pp))


def _select_kernel(kc_ref, kn_ref, bc_ref, bn_ref, key_ref, bias_ref, *, ch, kk):
    n_past = kc_ref.shape[1]
    key_ref[:, :n_past] = kc_ref[...]
    key_ref[:, n_past:] = kn_ref[...]
    _select_to_bias(key_ref, bias_ref, key_ref.shape[1] // ch, ch, kk)
    bc_ref[...] = bias_ref[:, :n_past]
    bn_ref[...] = bias_ref[:, n_past:]


def _select_sample(keys_cache, keys_new, kk):
    rows, n_past = keys_cache.shape
    s = n_past + keys_new.shape[1]
    ch = _largest_divisor(s, (512, 384, 256, 128))
    full = lambda c: pl.BlockSpec((rows, c), lambda i: (0, 0))
    return pl.pallas_call(
        functools.partial(_select_kernel, ch=ch, kk=kk),
        grid=(1,),
        in_specs=[full(n_past), full(PAGE_SIZE)],
        out_specs=[full(n_past), full(PAGE_SIZE)],
        out_shape=[jax.ShapeDtypeStruct((rows, n_past), _F32), jax.ShapeDtypeStruct((rows, PAGE_SIZE), _F32)],
        scratch_shapes=[pltpu.VMEM((rows, s), _I32), pltpu.VMEM((rows, s), _F32)],
        compiler_params=_params("arbitrary"),
        name="select_sample",
    )(keys_cache, keys_new)


def _dsa_sample_kernel(pt_ref, q_ref, kn_ref, vn_ref, *rest, ds, n_steps, pp):
    kc, vc = rest[:pp], rest[pp:2 * pp]
    bn_ref, bc_ref, z_ref, o_ref, m_ref, l_ref, acc_ref = rest[2 * pp:]
    p = pl.program_id(1)
    qs = q_ref[...]

    def step(key_pages, val_pages, bias, m_old, l_old, acc_old):
        s = jnp.concatenate([_stack_scores(qs, kh, ds) for kh in key_pages], axis=1)
        s = s + jnp.concatenate([bias] * DSA_HEADS, axis=0)
        m_new = jnp.maximum(m_old, jnp.max(s, axis=1, keepdims=True))
        alpha = jnp.exp2(m_old - m_new)
        pr = jnp.exp2(s - m_new)
        pv = _stack_pv(pr[:, :PAGE_SIZE], val_pages[0], ds)
        for k in range(1, len(val_pages)):
            pv = pv + _stack_pv(pr[:, k * PAGE_SIZE:(k + 1) * PAGE_SIZE], val_pages[k], ds)
        return m_new, l_old * alpha + jnp.sum(pr, axis=1, keepdims=True), acc_old * alpha + pv

    @pl.when(p == 0)
    def _():
        rows = DSA_HEADS * ds
        m, l, acc = step([_new_heads(kn_ref, DSA_HEADS)], [_new_heads(vn_ref, DSA_HEADS)], bn_ref[...],
                         jnp.full((rows, 1), NEG_BIAS, _F32), jnp.zeros((rows, 1), _F32),
                         jnp.zeros((rows, HEAD_DIM), _F32))
        m_ref[...] = m
        l_ref[...] = l
        acc_ref[...] = acc

    m, l, acc = step([_page_heads(r) for r in kc], [_page_heads(r) for r in vc], bc_ref[...],
                     m_ref[...], l_ref[...], acc_ref[...])
    m_ref[...] = m
    l_ref[...] = l
    acc_ref[...] = acc

    @pl.when(p == n_steps - 1)
    def _():
        o_ref[...] = (acc / l * _silu(z_ref[...])).astype(o_ref.dtype)


def _dsa_sample(qs, kn, vn, cache_k, cache_v, bias_cache, bias_new, zs, page_table, layer):
    db, rows, _ = qs.shape
    ds = rows // DSA_HEADS
    n_pages = page_table.shape[1]
    pp = _pages_per_step(n_pages, (8, 4, 2, 1))
    n_steps = n_pages // pp
    w = DSA_WIDTH
    seq = lambda r, c: pl.BlockSpec((None, r, c), lambda b, p, pt: (b, 0, 0))
    pages = [pl.BlockSpec((None, None, DSA_HEADS, PAGE_SIZE, HEAD_DIM),
                          lambda b, p, pt, k=k: (layer, pt[b, p * pp + k], 0, 0, 0)) for k in range(pp)]
    return pl.pallas_call(
        functools.partial(_dsa_sample_kernel, ds=ds, n_steps=n_steps, pp=pp),
        grid_spec=pltpu.PrefetchScalarGridSpec(
            num_scalar_prefetch=1,
            grid=(db, n_steps),
            in_specs=[seq(rows, HEAD_DIM), seq(ds, w), seq(ds, w)] + pages + pages
                     + [seq(ds, PAGE_SIZE),
                        pl.BlockSpec((None, ds, pp * PAGE_SIZE), lambda b, p, pt: (b, 0, p)),
                        seq(rows, HEAD_DIM)],
            out_specs=seq(rows, HEAD_DIM),
            scratch_shapes=[pltpu.VMEM((rows, 1), _F32), pltpu.VMEM((rows, 1), _F32),
                            pltpu.VMEM((rows, HEAD_DIM), _F32)]),
        out_shape=jax.ShapeDtypeStruct(qs.shape, _BF16),
        compiler_params=_params("parallel", "arbitrary"),
        name="dsa_sample",
    )(page_table, qs, kn, vn, *([cache_k] * pp), *([cache_v] * pp), bias_new, bias_cache, zs)


def _rope_tables(pos, reps):
    def base(half):
        inv_freq = jnp.power(jnp.float32(ROPE_THETA), -jnp.arange(half, dtype=_F32) / half)
        ang = pos.astype(_F32)[:, None] * inv_freq[None, :]
        return jnp.cos(ang), jnp.sin(ang)

    tile = lambda a: jnp.tile(a, (reps, 1))
    c, s = base(HEAD_DIM // 2)
    head = (tile(jnp.concatenate([c, c], axis=1)), tile(jnp.concatenate([-s, s], axis=1)))
    c, s = base(IDX_DIM // 2)
    z = jnp.zeros_like(s)
    c64, lo64, hi64 = (jnp.concatenate([c, c], axis=1), jnp.concatenate([-s, z], axis=1),
                       jnp.concatenate([z, s], axis=1))
    idx = tuple(tile(jnp.concatenate([a, a], axis=1)) for a in (c64, lo64, hi64))
    one, zero = jnp.ones_like(c64), jnp.zeros_like(c64)
    kiwi = (tile(jnp.concatenate([c64, one], axis=1)), tile(jnp.concatenate([lo64, zero], axis=1)),
            tile(jnp.concatenate([hi64, zero], axis=1)))
    return head, idx, kiwi


_ROPE128 = (LANES // 2,)
_ROPE64 = (LANES - IDX_DIM // 2, IDX_DIM // 2)

_OFF = dict(q_sb=0, k_sb=768, v_sb=1536, z_sb=2304, q_d=3072, k_d=3840, v_d=4608, z_d=5376, qi=6144)
_KIWI0 = 7168
_TAIL0 = _KIWI0 + IDX_DIM + IDX_HEADS


def _project_all(h, w_main, w_kiwi, w_tail, layer, tabs, d_model, depth, st, kv_dtype, seq_len):
    head_t, idx_t, kiwi_t = tabs
    pj = functools.partial(_proj, h)
    kv = lambda name: [("heads_major", st[name], depth, seq_len), ("flat", kv_dtype)]
    o = {}
    (o["q_sb"],) = pj(w_main, layer, _OFF["q_sb"], SB_WIDTH, [("flat", _BF16)], name="proj_q_sb")
    st["k_sb"], o["k_sb"] = pj(w_main, layer, _OFF["k_sb"], SB_WIDTH, kv("k_sb"), name="proj_k_sb")
    st["v_sb"], o["v_sb"] = pj(w_main, layer, _OFF["v_sb"], SB_WIDTH, kv("v_sb"), name="proj_v_sb")
    (o["z_sb"],) = pj(w_main, layer, _OFF["z_sb"], SB_WIDTH, [("flat", _F32)], name="proj_z_sb")
    (o["q_d"],) = pj(w_main, layer, _OFF["q_d"], DSA_WIDTH, [("flat", _BF16)], rope=(_ROPE128, head_t),
                     name="proj_q_d")
    st["k_d"], o["k_d"] = pj(w_main, layer, _OFF["k_d"], DSA_WIDTH, kv("k_d"), rope=(_ROPE128, head_t),
                             name="proj_k_d")
    st["v_d"], o["v_d"] = pj(w_main, layer, _OFF["v_d"], DSA_WIDTH, kv("v_d"), name="proj_v_d")
    (o["z_d"],) = pj(w_main, layer, _OFF["z_d"], DSA_WIDTH, [("flat", _F32)], name="proj_z_d")
    (o["qi"],) = pj(w_main, layer, _OFF["qi"], IDX_WIDTH, [("flat", _BF16)], rope=(_ROPE64, idx_t), name="proj_qi")
    o["kiwi"], o["ki_b"], st["ki"] = pj(
        w_kiwi, layer, 0, LANES,
        [("flat", _F32), ("lanes", IDX_DIM, _BF16), ("lanes_stacked", IDX_DIM, st["ki"], depth)],
        rope=(_ROPE64, kiwi_t), name="proj_kiwi")
    (o["q_m"],) = pj(w_tail, layer, 0, MEM_WIDTH, [("flat", _F32)], name="proj_q_m")
    (o["z_m"],) = pj(w_tail, layer, MEM_WIDTH, MEM_WIDTH, [("flat", _F32)], name="proj_z_m")
    (o["gates"],) = pj(w_tail, layer, 2 * MEM_WIDTH, 3 * d_model, [("flat", _F32)], name="proj_gates")
    return o


def kernel(x_prompt, x_sample, cache_sb_k, cache_sb_v, cache_dsa_k, cache_dsa_v, cache_idx_k, cache_mem_k,
           cache_mem_v, page_table, mem_prompt, norm_g, w_in, w_up_sb, w_up_dsa, w_up_mem, w_out, w_mem_k,
           w_mem_v, final_norm_g):
    b, t, d = x_prompt.shape
    db, ds, _ = x_sample.shape
    depth = w_in.shape[0]
    n_pool = cache_sb_k.shape[1]
    n_pages = page_table.shape[1]
    n_past = n_pages * PAGE_SIZE
    mem_len = mem_prompt.shape[1]
    topk_p = min(TOPK_MAX, t // 4)
    topk_s = min(TOPK_MAX, (n_past + ds) // 4)
    assert t % Q_BLOCK == 0 and w_in.shape[2] == _TAIL0 + 2 * MEM_WIDTH + 3 * d

    w_main = w_in[:, :, :_KIWI0].astype(_BF16)
    w_kiwi = jnp.pad(w_in[:, :, _KIWI0:_TAIL0], ((0, 0), (0, 0), (0, LANES - (_TAIL0 - _KIWI0)))).astype(_BF16)
    w_tail = w_in[:, :, _TAIL0:].astype(_BF16)
    w_up_sb_b, w_up_dsa_b, w_up_mem_b = (w.astype(_BF16) for w in (w_up_sb, w_up_dsa, w_up_mem))
    w_out_b = w_out.astype(_BF16)
    w_mem_k_b, w_mem_v_b = w_mem_k.astype(_BF16), w_mem_v.astype(_BF16)
    gains = jnp.concatenate([norm_g, final_norm_g[None]], axis=0)

    tabs_p = _rope_tables(jnp.arange(t), b)
    tabs_s = _rope_tables(n_past + jnp.arange(ds), db)

    n = b * t
    x = x_prompt.reshape(n, d)
    h = _rmsnorm(x, gains[0], _BF16)
    mem_b = mem_prompt.reshape(b * mem_len, d).astype(_BF16)
    st_p = dict.fromkeys(("k_sb", "v_sb", "k_d", "v_d", "ki", "mk", "mv"))
    for l in range(depth):
        o = _project_all(h, w_main, w_kiwi, w_tail, l, tabs_p, d, depth, st_p, _BF16, t)
        g_sb = _sb_prompt(o["q_sb"], o["k_sb"], o["v_sb"], o["z_sb"], b, t)
        bias = _idx_prompt(o["qi"], o["kiwi"], o["ki_b"], b, t, topk_p)
        g_d = _dsa_prompt(o["q_d"], o["k_d"], o["v_d"], bias, o["z_d"], b, t)
        st_p["mk"], mk = _proj(mem_b, w_mem_k_b, l, 0, MEM_WIDTH, [("heads", st_p["mk"], depth), ("flat", _BF16)],
                               name="proj_mem_k")
        st_p["mv"], mv = _proj(mem_b, w_mem_v_b, l, 0, MEM_WIDTH, [("heads", st_p["mv"], depth), ("flat", _BF16)],
                               name="proj_mem_v")
        g_m = _mem_attend(o["q_m"].reshape(b, t, MEM_WIDTH), mk.reshape(b, mem_len, MEM_WIDTH),
                          mv.reshape(b, mem_len, MEM_WIDTH), o["z_m"].reshape(b, t, MEM_WIDTH))
        mixed = _mix(g_sb, g_d, g_m.reshape(n, MEM_WIDTH), o["gates"], w_up_sb_b, w_up_dsa_b, w_up_mem_b, l)
        last = l == depth - 1
        x, h = _out_proj(x, mixed, w_out_b, l, gains[l + 1], _F32 if last else _BF16)
    y_prompt = h.reshape(b, t, d)

    ns = db * ds
    x = x_sample.reshape(ns, d)
    h = _rmsnorm(x, gains[0], _BF16)

    def stack(a, heads):
        c = a.shape[1] // heads
        return a.reshape(db, ds, heads, c).transpose(0, 2, 1, 3).reshape(db, heads * ds, c)

    def unstack(a, heads):
        c = a.shape[2]
        return a.reshape(db, heads, ds, c).transpose(0, 2, 1, 3).reshape(db * ds, heads * c)

    by_head = lambda c: jnp.transpose(c, (0, 1, 3, 2, 4))
    c_sb_k, c_sb_v, c_d_k, c_d_v = (by_head(c) for c in (cache_sb_k, cache_sb_v, cache_dsa_k, cache_dsa_v))
    c_idx = jnp.transpose(cache_idx_k, (0, 1, 3, 2))

    st_s = dict.fromkeys(("k_sb", "v_sb", "k_d", "v_d", "ki"))
    for l in range(depth):
        o = _project_all(h, w_main, w_kiwi, w_tail, l, tabs_s, d, depth, st_s, _F32, ds)
        ki = o["kiwi"][:, :IDX_DIM]
        wi = o["kiwi"][:, IDX_DIM:IDX_DIM + IDX_HEADS]
        g_sb = _sb_sample(stack(o["q_sb"], SB_HEADS), o["k_sb"].reshape(db, ds, SB_WIDTH),
                          o["v_sb"].reshape(db, ds, SB_WIDTH), c_sb_k, c_sb_v,
                          stack(o["z_sb"], SB_HEADS), page_table, l)
        keys_c, keys_n = _idx_sample(stack(o["qi"], IDX_HEADS), stack(wi, IDX_HEADS),
                                     ki.reshape(db, ds, IDX_DIM), c_idx, page_table, l)
        bias_c, bias_n = _select_sample(keys_c.reshape(ns, n_past), keys_n.reshape(ns, PAGE_SIZE), topk_s)
        g_d = _dsa_sample(stack(o["q_d"], DSA_HEADS), o["k_d"].reshape(db, ds, DSA_WIDTH),
                          o["v_d"].reshape(db, ds, DSA_WIDTH), c_d_k, c_d_v,
                          bias_c.reshape(db, ds, n_past), bias_n.reshape(db, ds, PAGE_SIZE),
                          stack(o["z_d"], DSA_HEADS), page_table, l)
        g_m = _mem_attend(o["q_m"].reshape(db, ds, MEM_WIDTH), cache_mem_k, cache_mem_v,
                          o["z_m"].reshape(db, ds, MEM_WIDTH), layer=l)
        mixed = _mix(unstack(g_sb, SB_HEADS), unstack(g_d, DSA_HEADS), g_m.reshape(ns, MEM_WIDTH), o["gates"],
                     w_up_sb_b, w_up_dsa_b, w_up_mem_b, l)
        last = l == depth - 1
        x, h = _out_proj(x, mixed, w_out_b, l, gains[l + 1], _F32 if last else _BF16)
    y_sample = h.reshape(db, ds, d)

    kv = lambda a: jnp.transpose(a, (0, 1, 3, 2, 4))
    hp = lambda a: a.reshape((depth, b, -1) + a.shape[2:])
    hs = lambda a: a.reshape((depth, db, ds) + a.shape[2:])
    return (y_prompt, y_sample,
            kv(st_p["k_sb"]), kv(st_p["v_sb"]), kv(st_p["k_d"]), kv(st_p["v_d"]), hp(st_p["ki"]),
            hp(st_p["mk"]), hp(st_p["mv"]),
            kv(st_s["k_sb"]), kv(st_s["v_sb"]), kv(st_s["k_d"]), kv(st_s["v_d"]), hs(st_s["ki"]))
```

```python
import functools
import math

import numpy as np
import jax
import jax.numpy as jnp
from jax import lax
from jax.experimental import pallas as pl
from jax.experimental.pallas import tpu as pltpu

HEAD_DIM = 128
SB_HEADS = 6
DSA_HEADS = 6
MEM_HEADS = 4
SB_WIDTH = SB_HEADS * HEAD_DIM
DSA_WIDTH = DSA_HEADS * HEAD_DIM
MEM_WIDTH = MEM_HEADS * HEAD_DIM
IDX_HEADS = 16
IDX_DIM = 64
IDX_WIDTH = IDX_HEADS * IDX_DIM
PAGE_SIZE = 128
TOPK_MAX = 256
Q_BLOCK = 128
ROPE_THETA = 10000.0
NORM_EPS = 1e-6
ATTN_SCALE = HEAD_DIM ** -0.5
LOG2E = math.log2(math.e)
ATTN_SCALE2 = ATTN_SCALE * LOG2E
IDX_SCALE = (IDX_HEADS * IDX_DIM) ** -0.5

SB_DEAD_LOG2 = -160.0
LANES = 128
NEG_BIAS = -1e30
INT32_MIN = -2 ** 31
KEY_NEG_INF = -2139095041
VMEM_LIMIT = 56 * 2 ** 20

_F32 = jnp.float32
_BF16 = jnp.bfloat16
_I32 = jnp.int32
_NT = (((1,), (1,)), ((), ()))


def _params(*sem):
    return pltpu.CompilerParams(dimension_semantics=sem, vmem_limit_bytes=VMEM_LIMIT)


def _sigmoid(x):
    return 1.0 / (1.0 + jnp.exp(-x))


def _silu(x):
    return x * _sigmoid(x)


def _softplus2(t):
    return jnp.maximum(t, 0.0) + jnp.log2(1.0 + jnp.exp2(-jnp.abs(t)))


def _split_bf16(x):
    hi = x.astype(_BF16)
    lo = (x - hi.astype(_F32)).astype(_BF16)
    return hi, lo


def _sortable_key(s):
    b = lax.bitcast_convert_type(s, _I32)
    return b ^ ((b >> 31) & 0x7FFFFFFF)


def _largest_divisor(n, candidates):
    for c in candidates:
        if n % c == 0:
            return c
    raise ValueError(f"no tile in {candidates} divides {n}")


def _rms_kernel(x_ref, g_ref, o_ref):
    x = x_ref[...]
    y = x * lax.rsqrt(jnp.mean(x * x, axis=-1, keepdims=True) + NORM_EPS)
    o_ref[...] = (y * g_ref[...]).astype(o_ref.dtype)


def _rmsnorm(x, g, out_dtype):
    n, d = x.shape
    tm = _largest_divisor(n, (512, 256, 128, 64))
    return pl.pallas_call(
        _rms_kernel,
        grid=(n // tm,),
        in_specs=[pl.BlockSpec((tm, d), lambda i: (i, 0)),
                  pl.BlockSpec((1, d), lambda i: (0, 0))],
        out_specs=pl.BlockSpec((tm, d), lambda i: (i, 0)),
        out_shape=jax.ShapeDtypeStruct((n, d), out_dtype),
        compiler_params=_params("parallel"),
        name="rmsnorm",
    )(x, g.reshape(1, d))


def _proj_kernel(*refs, shifts, kinds, n_alias):
    h_ref, w_ref = refs[0], refs[1]
    n_tab = 1 + len(shifts) if shifts else 0
    tabs = refs[2:2 + n_tab]
    outs = refs[2 + n_tab + n_alias:]
    z = jnp.dot(h_ref[...], w_ref[...], preferred_element_type=_F32)
    tn = z.shape[1]
    for g in range(tn // LANES):
        zg = z[:, g * LANES:(g + 1) * LANES]
        if shifts:
            r = zg * tabs[0][...]
            for t, sh in enumerate(shifts):
                r = r + pltpu.roll(zg, sh, 1) * tabs[1 + t][...]
            zg = r
        for kind, o_ref in zip(kinds, outs):
            if kind == "flat":
                o_ref[:, g * LANES:(g + 1) * LANES] = zg.astype(o_ref.dtype)
            elif kind == "flat_ones":
                o_ref[:, 2 * g * LANES:(2 * g + 1) * LANES] = zg.astype(o_ref.dtype)
                o_ref[:, (2 * g + 1) * LANES:(2 * g + 2) * LANES] = jnp.ones(zg.shape, o_ref.dtype)
            elif kind == "heads":
                o_ref[:, g, :] = zg
            elif kind == "heads_major":
                if len(o_ref.shape) == 3:
                    o_ref[g] = zg
                else:
                    gr = o_ref.shape[2]
                    for b in range(o_ref.shape[0]):
                        o_ref[b, g] = zg[b * gr:(b + 1) * gr]
            else:
                o_ref[...] = zg[:, :o_ref.shape[1]].astype(o_ref.dtype)


def _proj(h, w, layer, col0, width, outs, rope=None, name="proj"):
    n, d = h.shape
    tm = _largest_divisor(n, (1024, 512, 256, 128, 64))
    tn = _largest_divisor(width, (1024, 768, 512, 256, 128))
    assert col0 % tn == 0
    cb = col0 // tn
    shifts, tables = rope if rope else ((), ())
    in_specs = [pl.BlockSpec((tm, d), lambda j, i: (i, 0)),
                pl.BlockSpec((None, d, tn), lambda j, i: (layer, 0, cb + j))]
    operands = [h, w]
    for t in tables:
        assert t.shape == (n, LANES)
        in_specs.append(pl.BlockSpec((tm, LANES), lambda j, i: (i, 0)))
        operands.append(t)
    kinds, out_specs, out_shape, aliases = [], [], [], {}
    prevs = []
    for req in outs:
        kind = req[0]
        if kind == "flat":
            out_specs.append(pl.BlockSpec((tm, tn), lambda j, i: (i, j)))
            out_shape.append(jax.ShapeDtypeStruct((n, width), req[1]))
        elif kind == "flat_ones":
            out_specs.append(pl.BlockSpec((tm, 2 * tn), lambda j, i: (i, j)))
            out_shape.append(jax.ShapeDtypeStruct((n, 2 * width), req[1]))
        elif kind == "lanes":
            assert width == LANES
            out_specs.append(pl.BlockSpec((tm, req[1]), lambda j, i: (i, 0)))
            out_shape.append(jax.ShapeDtypeStruct((n, req[1]), req[2]))
        elif kind == "heads":
            assert tn == width
            nh = width // LANES
            out_specs.append(pl.BlockSpec((None, tm, nh, LANES), lambda j, i: (layer, i, 0, 0)))
            out_shape.append(jax.ShapeDtypeStruct((req[2], n, nh, LANES), _F32))
            prevs.append((len(kinds), req[1]))
        elif kind == "heads_major":
            assert tn == width
            nh = width // LANES
            gr = req[3]
            if tm <= gr:
                per = gr // tm
                out_specs.append(pl.BlockSpec((None, None, nh, tm, LANES),
                                              lambda j, i, per=per: (layer, i // per, 0, i % per, 0)))
            else:
                out_specs.append(pl.BlockSpec((None, tm // gr, nh, gr, LANES), lambda j, i: (layer, i, 0, 0, 0)))
            out_shape.append(jax.ShapeDtypeStruct((req[2], n // gr, nh, gr, LANES), _F32))
            prevs.append((len(kinds), req[1]))
        elif kind == "lanes_stacked":
            assert width == LANES
            out_specs.append(pl.BlockSpec((None, tm, req[1]), lambda j, i: (layer, i, 0)))
            out_shape.append(jax.ShapeDtypeStruct((req[3], n, req[1]), _F32))
            prevs.append((len(kinds), req[2]))
        else:
            raise ValueError(kind)
        kinds.append(kind)
    n_alias = 0
    for out_idx, prev in prevs:
        if prev is not None:
            aliases[len(operands)] = out_idx
            in_specs.append(pl.BlockSpec(memory_space=pl.ANY))
            operands.append(prev)
            n_alias += 1
    return pl.pallas_call(
        functools.partial(_proj_kernel, shifts=tuple(shifts), kinds=tuple(kinds), n_alias=n_alias),
        grid=(width // tn, n // tm),
        in_specs=in_specs,
        out_specs=out_specs,
        out_shape=out_shape,
        input_output_aliases=aliases,
        compiler_params=_params("parallel", "parallel"),
        name=name,
    )(*operands)


def _cumsum_matrix(t):
    r = np.arange(2 * t) % t
    c = np.arange(2 * t)
    m = np.where(c[None, :] < t, r[:, None] > c[None, :], True)
    return jnp.asarray(m, dtype=_BF16)


def _sb_prompt_kernel(q_ref, k_ref, v_ref, z_ref, u_ref, o_ref, carry_ref, acc_ref, *, tq):
    i = pl.program_id(1)
    heads = SB_HEADS
    hsl = [slice(h * HEAD_DIM, (h + 1) * HEAD_DIM) for h in range(heads)]
    rsl = [slice(h * tq, (h + 1) * tq) for h in range(heads)]

    def key_tiles(off, nt, before):
        n = heads * tq
        s = jnp.concatenate(
            [lax.dot_general(q_ref[:, hsl[h]], k_ref[pl.ds(off + (nt - 1 - j) * tq, tq), hsl[h]], _NT,
                             preferred_element_type=_F32)
             for j in range(nt) for h in range(heads)], axis=0) * ATTN_SCALE2
        sp = _softplus2(s)
        lk = -sp
        if before is not None:
            lk = jnp.where(before, lk, 0.0)
        hi, lo = _split_bf16(lk)
        cr = jnp.dot(jnp.concatenate([hi, lo], axis=1), u_ref[...], preferred_element_type=_F32)
        carry = carry_ref[...]
        for j in range(nt):
            r = slice(j * n, (j + 1) * n)
            a = jnp.exp2(s[r] - sp[r] + carry + cr[r, :tq])
            if before is not None:
                a = jnp.where(before, a, 0.0)
            carry = carry + cr[r, tq:]
            ab = a.astype(_BF16)
            for h in range(heads):
                acc_ref[rsl[h], :] += jnp.dot(ab[rsl[h]], v_ref[pl.ds(off + (nt - 1 - j) * tq, tq), hsl[h]],
                                              preferred_element_type=_F32)
        carry_ref[...] = carry

    carry_ref[...] = jnp.zeros_like(carry_ref)
    acc_ref[...] = jnp.zeros_like(acc_ref)
    row = lax.broadcasted_iota(_I32, (heads * tq, tq), 0) & (tq - 1)
    col = lax.broadcasted_iota(_I32, (heads * tq, tq), 1)
    key_tiles(pl.multiple_of(i * tq, tq), 1, col < row)

    def live():
        return jnp.max(carry_ref[...]) > SB_DEAD_LOG2

    def body(c):
        key_tiles(pl.multiple_of((i - 2 - 2 * c[0]) * tq, tq), 2, None)
        return c[0] + 1, live()

    _, alive = lax.while_loop(lambda c: jnp.logical_and(c[0] < i // 2, c[1]), body, (jnp.int32(0), live()))

    @pl.when(jnp.logical_and(i % 2 == 1, alive))
    def _():
        key_tiles(0, 1, None)
    for h in range(heads):
        o_ref[:, hsl[h]] = (acc_ref[rsl[h], :] * _silu(z_ref[:, hsl[h]])).astype(o_ref.dtype)


def _sb_prompt(q, k, v, z, b, t):
    tq = Q_BLOCK
    nq = t // tq
    w = SB_WIDTH
    u = _cumsum_matrix(tq)
    return pl.pallas_call(
        functools.partial(_sb_prompt_kernel, tq=tq),
        grid=(b, nq),
        in_specs=[pl.BlockSpec((tq, w), lambda bi, i: (bi * nq + i, 0)),
                  pl.BlockSpec((t, w), lambda bi, i: (bi, 0)),
                  pl.BlockSpec((t, w), lambda bi, i: (bi, 0)),
                  pl.BlockSpec((tq, w), lambda bi, i: (bi * nq + i, 0)),
                  pl.BlockSpec((2 * tq, 2 * tq), lambda bi, i: (0, 0))],
        out_specs=pl.BlockSpec((tq, w), lambda bi, i: (bi * nq + i, 0)),
        out_shape=jax.ShapeDtypeStruct(q.shape, _BF16),
        scratch_shapes=[pltpu.VMEM((SB_HEADS * tq, tq), _F32), pltpu.VMEM((SB_HEADS * tq, HEAD_DIM), _F32)],
        compiler_params=_params("parallel", "arbitrary"),
        name="sb_prompt",
    )(q, k, v, z, u)


def _select_to_bias(key_ref, bias_ref, nch, ch, kk):
    rows = key_ref.shape[0]

    def count(pred):
        def body(c, part):
            off = pl.multiple_of(c * ch, ch)
            m = jnp.where(pred(key_ref[:, pl.ds(off, ch)], off), 1, 0).astype(_I32)
            for g in range(ch // LANES):
                part = part + m[:, g * LANES:(g + 1) * LANES]
            return part
        part = lax.fori_loop(0, nch, body, jnp.zeros((rows, LANES), _I32))
        return jnp.sum(part, axis=1, keepdims=True)

    def bit_body(it, u):
        cand_u = u | lax.shift_left(jnp.int32(1), 31 - it)
        cand = cand_u ^ jnp.int32(INT32_MIN)
        cnt = count(lambda kc, off: kc >= cand)
        return jnp.where(cnt >= kk, cand_u, u)

    u = lax.fori_loop(0, 32, bit_body, jnp.zeros((rows, 1), _I32))
    thr = u ^ jnp.int32(INT32_MIN)
    cnt_gt = count(lambda kc, off: kc > thr)
    cnt_ge = count(lambda kc, off: kc >= thr)
    need = kk - cnt_gt
    excess = jnp.where((cnt_ge > kk) & (thr > KEY_NEG_INF), 1, 0).astype(_I32)
    nbits = max(1, int(math.ceil(math.log2(key_ref.shape[1] + 1))))

    def tie_cut(_):
        def q_body(it, q):
            cand = q | lax.shift_left(jnp.int32(1), nbits - 1 - it)

            def pred(kc, off):
                idx = off + lax.broadcasted_iota(_I32, kc.shape, 1)
                return (kc == thr) & (idx < cand)
            return jnp.where(count(pred) < need, cand, q)
        return lax.fori_loop(0, nbits, q_body, jnp.zeros((rows, 1), _I32))

    def no_cut(_):
        return jnp.full((rows, 1), 2 ** 30, _I32)

    qcut = lax.cond(jnp.max(excess) > 0, tie_cut, no_cut, 0)

    def write(c, _):
        off = pl.multiple_of(c * ch, ch)
        kc = key_ref[:, pl.ds(off, ch)]
        idx = off + lax.broadcasted_iota(_I32, kc.shape, 1)
        sel = (kc > thr) | ((kc == thr) & (idx <= qcut))
        ok = sel & (kc > KEY_NEG_INF)
        bias_ref[:, pl.ds(off, ch)] = jnp.where(ok, 0.0, NEG_BIAS).astype(_F32)
        return 0

    lax.fori_loop(0, nch, write, 0)


def _idx_prompt_kernel(qi_ref, wi_ref, ki_ref, bias_ref, key_ref, *, tq, ch, kk):
    i = pl.program_id(1)
    nch = (i * tq + tq + ch - 1) // ch
    qi = qi_ref[...]
    wi = wi_ref[...]
    rowpos = i * tq + lax.broadcasted_iota(_I32, (tq, ch), 0)

    def score(c, _):
        off = pl.multiple_of(c * ch, ch)
        kc = ki_ref[pl.ds(off, ch), :]
        acc = jnp.zeros((tq, ch), _F32)
        for h in range(IDX_HEADS):
            r = lax.dot_general(qi[:, h * IDX_DIM:(h + 1) * IDX_DIM], kc, _NT,
                                preferred_element_type=_F32)
            acc = acc + jnp.maximum(r, 0.0) * wi[:, IDX_DIM + h:IDX_DIM + h + 1]
        sc = acc * IDX_SCALE
        sc = jnp.where(sc == 0.0, 0.0, sc)
        colpos = off + lax.broadcasted_iota(_I32, (tq, ch), 1)
        sc = jnp.where(colpos <= rowpos, sc, -jnp.inf)
        key_ref[:, pl.ds(off, ch)] = _sortable_key(sc)
        return 0

    lax.fori_loop(0, nch, score, 0)
    bias_ref[...] = jnp.full(bias_ref.shape, NEG_BIAS, _F32)
    _select_to_bias(key_ref, bias_ref, nch, ch, kk)


def _idx_prompt(qi, kiwi, ki, b, t, kk):
    tq = Q_BLOCK
    nq = t // tq
    ch = _largest_divisor(t, (512, 256, 128))
    return pl.pallas_call(
        functools.partial(_idx_prompt_kernel, tq=tq, ch=ch, kk=kk),
        grid=(b, nq),
        in_specs=[pl.BlockSpec((tq, IDX_WIDTH), lambda bi, i: (bi * nq + i, 0)),
                  pl.BlockSpec((tq, LANES), lambda bi, i: (bi * nq + i, 0)),
                  pl.BlockSpec((t, IDX_DIM), lambda bi, i: (bi, 0))],
        out_specs=pl.BlockSpec((tq, t), lambda bi, i: (bi * nq + i, 0)),
        out_shape=jax.ShapeDtypeStruct((b * t, t), _F32),
        scratch_shapes=[pltpu.VMEM((tq, t), _I32)],
        compiler_params=_params("parallel", "arbitrary"),
        name="idx_select_prompt",
    )(qi, kiwi, ki)


def _dsa_prompt_kernel(q_ref, k_ref, v_ref, bias_ref, z_ref, o_ref, m_ref, l_ref, acc_ref, *, tq, ch):
    i = pl.program_id(1)
    nch = (i * tq + tq + ch - 1) // ch
    heads = DSA_HEADS
    hsl = [slice(h * HEAD_DIM, (h + 1) * HEAD_DIM) for h in range(heads)]
    rsl = [slice(h * tq, (h + 1) * tq) for h in range(heads)]
    m_ref[...] = jnp.full(m_ref.shape, NEG_BIAS, _F32)
    l_ref[...] = jnp.zeros_like(l_ref)
    acc_ref[...] = jnp.zeros_like(acc_ref)

    def body(c, _):
        off = pl.multiple_of(c * ch, ch)
        bias = bias_ref[:, pl.ds(off, ch)]
        s = jnp.concatenate(
            [lax.dot_general(q_ref[:, hsl[h]], k_ref[pl.ds(off, ch), hsl[h]], _NT, preferred_element_type=_F32)
             for h in range(heads)], axis=0) * (ATTN_SCALE * LOG2E) + jnp.concatenate([bias] * heads, axis=0)
        m_old = m_ref[...]
        m_new = jnp.maximum(m_old, jnp.max(s, axis=1, keepdims=True))
        alpha = jnp.exp2(m_old - m_new)
        pb = jnp.exp2(s - m_new).astype(_BF16)
        for h in range(heads):
            pv = jnp.dot(pb[rsl[h]], v_ref[pl.ds(off, ch), 2 * h * HEAD_DIM:(2 * h + 2) * HEAD_DIM],
                         preferred_element_type=_F32)
            acc_ref[rsl[h], :] = acc_ref[rsl[h], :] * alpha[rsl[h]] + pv[:, :HEAD_DIM]
            l_ref[rsl[h], :] = l_ref[rsl[h], :] * alpha[rsl[h]] + pv[:, HEAD_DIM:]
        m_ref[...] = m_new
        return 0

    lax.fori_loop(0, nch, body, 0)
    for h in range(heads):
        o_ref[:, hsl[h]] = (acc_ref[rsl[h], :] / l_ref[rsl[h], :] * _silu(z_ref[:, hsl[h]])).astype(o_ref.dtype)


def _dsa_prompt(q, k, v, bias, z, b, t):
    tq = Q_BLOCK
    nq = t // tq
    ch = _largest_divisor(t, (1024, 512, 256, 128))
    w = DSA_WIDTH
    return pl.pallas_call(
        functools.partial(_dsa_prompt_kernel, tq=tq, ch=ch),
        grid=(b, nq),
        in_specs=[pl.BlockSpec((tq, w), lambda bi, i: (bi * nq + i, 0)),
                  pl.BlockSpec((t, w), lambda bi, i: (bi, 0)),
                  pl.BlockSpec((t, 2 * w), lambda bi, i: (bi, 0)),
                  pl.BlockSpec((tq, t), lambda bi, i: (bi * nq + i, 0)),
                  pl.BlockSpec((tq, w), lambda bi, i: (bi * nq + i, 0))],
        out_specs=pl.BlockSpec((tq, w), lambda bi, i: (bi * nq + i, 0)),
        out_shape=jax.ShapeDtypeStruct(q.shape, _BF16),
        scratch_shapes=[pltpu.VMEM((DSA_HEADS * tq, 1), _F32), pltpu.VMEM((DSA_HEADS * tq, LANES), _F32),
                        pltpu.VMEM((DSA_HEADS * tq, HEAD_DIM), _F32)],
        compiler_params=_params("parallel", "arbitrary"),
        name="dsa_prompt",
    )(q, k, v, bias, z)


def _mem_kernel(q_ref, mk_ref, mv_ref, z_ref, o_ref):
    tm = q_ref.shape[0]
    pad = (-tm) % 16
    by_head = len(mk_ref.shape) == 3
    for h in range(MEM_HEADS):
        hs = slice(h * HEAD_DIM, (h + 1) * HEAD_DIM)
        mk = mk_ref[:, h, :] if by_head else mk_ref[:, hs]
        mv = mv_ref[:, h, :] if by_head else mv_ref[:, hs]
        qh = q_ref[:, hs].astype(_F32)
        if pad:
            qh = jnp.concatenate([qh, jnp.zeros((pad, HEAD_DIM), _F32)], axis=0)
        s = lax.dot_general(qh.astype(_BF16), mk.astype(_BF16), _NT, preferred_element_type=_F32) * ATTN_SCALE2
        p = jnp.exp2(s - jnp.max(s, axis=1, keepdims=True))
        p = p / jnp.sum(p, axis=1, keepdims=True)
        o = jnp.dot(p.astype(_BF16), mv.astype(_BF16), preferred_element_type=_F32)
        o_ref[:, hs] = (o[:tm] * _silu(z_ref[:, hs])).astype(o_ref.dtype)


def _mem_attend(q, mk, mv, z, layer=None):
    g, r, w = q.shape
    tm = _largest_divisor(r, (512, 256, 128, 64, 8))
    if layer is None:
        m = mk.shape[1]
        mem = pl.BlockSpec((None, m, w), lambda gi, i: (gi, 0, 0))
    else:
        m = mk.shape[2]
        mem = pl.BlockSpec((None, None, m, MEM_HEADS, HEAD_DIM), lambda gi, i: (layer, gi, 0, 0, 0))
    return pl.pallas_call(
        _mem_kernel,
        grid=(g, r // tm),
        in_specs=[pl.BlockSpec((None, tm, w), lambda gi, i: (gi, i, 0)),
                  mem, mem,
                  pl.BlockSpec((None, tm, w), lambda gi, i: (gi, i, 0))],
        out_specs=pl.BlockSpec((None, tm, w), lambda gi, i: (gi, i, 0)),
        out_shape=jax.ShapeDtypeStruct((g, r, w), _BF16),
        compiler_params=_params("parallel", "parallel"),
        name="mem_attend",
    )(q, mk, mv, z)


def _mix_kernel(gs_ref, gd_ref, gm_ref, g1_ref, g2_ref, g3_ref, w1_ref, w2_ref, w3_ref, o_ref):
    a = jnp.dot(gs_ref[...], w1_ref[...], preferred_element_type=_F32)
    c = jnp.dot(gd_ref[...], w2_ref[...], preferred_element_type=_F32)
    m = jnp.dot(gm_ref[...], w3_ref[...], preferred_element_type=_F32)
    mixed = _sigmoid(g1_ref[...]) * a + _sigmoid(g2_ref[...]) * c + _sigmoid(g3_ref[...]) * m
    o_ref[...] = mixed.astype(o_ref.dtype)


def _mix(gs, gd, gm, gates, w1, w2, w3, layer):
    n = gs.shape[0]
    d = w1.shape[2]
    tm = _largest_divisor(n, (256, 128, 64))
    row = lambda w: pl.BlockSpec((tm, w), lambda i: (i, 0))
    wspec = lambda w: pl.BlockSpec((None, w.shape[1], d), lambda i: (layer, 0, 0))
    return pl.pallas_call(
        _mix_kernel,
        grid=(n // tm,),
        in_specs=[row(SB_WIDTH), row(DSA_WIDTH), row(MEM_WIDTH),
                  pl.BlockSpec((tm, d), lambda i: (i, 0)),
                  pl.BlockSpec((tm, d), lambda i: (i, 1)),
                  pl.BlockSpec((tm, d), lambda i: (i, 2)),
                  wspec(w1), wspec(w2), wspec(w3)],
        out_specs=pl.BlockSpec((tm, d), lambda i: (i, 0)),
        out_shape=jax.ShapeDtypeStruct((n, d), _BF16),
        compiler_params=_params("parallel"),
        name="mix",
    )(gs, gd, gm, gates, gates, gates, w1, w2, w3)


def _out_kernel(x_ref, m_ref, w_ref, g_ref, xo_ref, ho_ref):
    xn = x_ref[...] + jnp.dot(m_ref[...], w_ref[...], preferred_element_type=_F32)
    xo_ref[...] = xn
    y = xn * lax.rsqrt(jnp.mean(xn * xn, axis=-1, keepdims=True) + NORM_EPS)
    ho_ref[...] = (y * g_ref[...]).astype(ho_ref.dtype)


def _out_proj(x, mixed, w, layer, g, h_dtype):
    n, d = x.shape
    tm = _largest_divisor(n, (512, 256, 128, 64))
    return pl.pallas_call(
        _out_kernel,
        grid=(n // tm,),
        in_specs=[pl.BlockSpec((tm, d), lambda i: (i, 0)),
                  pl.BlockSpec((tm, d), lambda i: (i, 0)),
                  pl.BlockSpec((None, d, d), lambda i: (layer, 0, 0)),
                  pl.BlockSpec((1, d), lambda i: (0, 0))],
        out_specs=[pl.BlockSpec((tm, d), lambda i: (i, 0)),
                   pl.BlockSpec((tm, d), lambda i: (i, 0))],
        out_shape=[jax.ShapeDtypeStruct((n, d), _F32), jax.ShapeDtypeStruct((n, d), h_dtype)],
        compiler_params=_params("parallel"),
        name="out_proj",
    )(x, mixed, w, g.reshape(1, d))


def _head_rows(x, ds, h):
    return x[h * ds:(h + 1) * ds]


def _stack_scores(qs, key_heads, ds):
    parts = []
    for h, kh in enumerate(key_heads):
        r = lax.dot_general(qs, kh, _NT, preferred_element_type=_F32)
        parts.append(_head_rows(r, ds, h))
    return jnp.concatenate(parts, axis=0) * ATTN_SCALE2


def _stack_pv(p, val_heads, ds):
    pb = p.astype(_BF16)
    parts = []
    for h, vh in enumerate(val_heads):
        r = jnp.dot(pb, vh, preferred_element_type=_F32)
        parts.append(_head_rows(r, ds, h))
    return jnp.concatenate(parts, axis=0)


def _pad_rows(x, rows):
    return jnp.concatenate([x, jnp.zeros((rows - x.shape[0], x.shape[1]), x.dtype)], axis=0)


def _page_heads(ref):
    return [ref[h].astype(_BF16) for h in range(ref.shape[0])]


def _new_heads(ref, heads):
    x = _pad_rows(ref[...], PAGE_SIZE).astype(_BF16)
    return [x[:, h * HEAD_DIM:(h + 1) * HEAD_DIM] for h in range(heads)]


def _pages_per_step(n_pages, candidates):
    return _largest_divisor(n_pages, candidates)


def _sb_sample_kernel(pt_ref, q_ref, kn_ref, vn_ref, z_ref, u_ref, kc_hbm, vc_hbm, o_ref,
                      kbuf, vbuf, sem, carry_ref, acc_ref, *, ds, n_pages, pp, layer):
    b = pl.program_id(0)
    n_steps = n_pages // pp
    rows = SB_HEADS * ds
    qs = q_ref[...]

    def page_copies(step, slot):
        cps = []
        for k in range(pp):
            page = pt_ref[b, n_pages - 1 - (step * pp + k)]
            cps.append(pltpu.make_async_copy(kc_hbm.at[layer, page], kbuf.at[slot, k], sem.at[slot, 0, k]))
            cps.append(pltpu.make_async_copy(vc_hbm.at[layer, page], vbuf.at[slot, k], sem.at[slot, 1, k]))
        return cps

    def start(step, slot):
        for cp in page_copies(step, slot):
            cp.start()

    def wait(step, slot):
        for cp in page_copies(step, slot):
            cp.wait()

    start(0, 0)

    def page(key_heads, val_heads, before, carry, acc):
        s = _stack_scores(qs, key_heads, ds)
        sp = _softplus2(s)
        lk = -sp
        if before is not None:
            lk = jnp.where(before, lk, 0.0)
        hi, lo = _split_bf16(lk)
        cr = jnp.dot(jnp.concatenate([hi, lo], axis=1), u_ref[...], preferred_element_type=_F32)
        a = jnp.exp2(s - sp + carry + cr[:, :PAGE_SIZE])
        if before is not None:
            a = jnp.where(before, a, 0.0)
        return carry + cr[:, PAGE_SIZE:], acc + _stack_pv(a, val_heads, ds)

    qrow = lax.broadcasted_iota(_I32, (rows, PAGE_SIZE), 0) % ds
    col = lax.broadcasted_iota(_I32, (rows, PAGE_SIZE), 1)
    carry, acc = page(_new_heads(kn_ref, SB_HEADS), _new_heads(vn_ref, SB_HEADS), col < qrow,
                      jnp.zeros((rows, PAGE_SIZE), _F32), jnp.zeros((rows, HEAD_DIM), _F32))
    carry_ref[...] = carry
    acc_ref[...] = acc

    def live():
        return jnp.max(carry_ref[...]) > SB_DEAD_LOG2

    def body(c):
        step = c[0]
        slot = step % 2

        @pl.when(step + 1 < n_steps)
        def _():
            start(step + 1, 1 - slot)

        wait(step, slot)
        carry, acc = carry_ref[...], acc_ref[...]
        s = jnp.concatenate([_stack_scores(qs, _page_heads(kbuf.at[slot, k]), ds) for k in range(pp)], axis=0)
        sp = _softplus2(s)
        hi, lo = _split_bf16(-sp)
        cr = jnp.dot(jnp.concatenate([hi, lo], axis=1), u_ref[...], preferred_element_type=_F32)
        weights = []
        for k in range(pp):
            r = slice(k * rows, (k + 1) * rows)
            weights.append(jnp.exp2(s[r] - sp[r] + carry + cr[r, :PAGE_SIZE]))
            carry = carry + cr[r, PAGE_SIZE:]
        for k in range(pp):
            acc = acc + _stack_pv(weights[k], _page_heads(vbuf.at[slot, k]), ds)
        carry_ref[...] = carry
        acc_ref[...] = acc
        return step + 1, live()

    done, _ = lax.while_loop(lambda c: jnp.logical_and(c[0] < n_steps, c[1]), body, (jnp.int32(0), live()))

    @pl.when(done < n_steps)
    def _():
        wait(done, done % 2)

    o_ref[...] = (acc_ref[...] * _silu(z_ref[...])).astype(o_ref.dtype)


def _sb_sample(qs, kn, vn, cache_k, cache_v, zs, page_table, layer):
    db, rows, _ = qs.shape
    ds = rows // SB_HEADS
    n_pages = page_table.shape[1]
    pp = _pages_per_step(n_pages, (4, 2, 1))
    w = SB_WIDTH
    u = _cumsum_matrix(PAGE_SIZE)
    seq = lambda r, c: pl.BlockSpec((None, r, c), lambda b, pt: (b, 0, 0))
    hbm = pl.BlockSpec(memory_space=pl.ANY)
    page_buf = pltpu.VMEM((2, pp, SB_HEADS, PAGE_SIZE, HEAD_DIM), _F32)
    return pl.pallas_call(
        functools.partial(_sb_sample_kernel, ds=ds, n_pages=n_pages, pp=pp, layer=layer),
        grid_spec=pltpu.PrefetchScalarGridSpec(
            num_scalar_prefetch=1,
            grid=(db,),
            in_specs=[seq(rows, HEAD_DIM), seq(ds, w), seq(ds, w), seq(rows, HEAD_DIM),
                      pl.BlockSpec((2 * PAGE_SIZE, 2 * PAGE_SIZE), lambda b, pt: (0, 0)), hbm, hbm],
            out_specs=seq(rows, HEAD_DIM),
            scratch_shapes=[page_buf, page_buf, pltpu.SemaphoreType.DMA((2, 2, pp)),
                            pltpu.VMEM((rows, PAGE_SIZE), _F32), pltpu.VMEM((rows, HEAD_DIM), _F32)]),
        out_shape=jax.ShapeDtypeStruct(qs.shape, _BF16),
        compiler_params=_params("arbitrary"),
        name="sb_sample",
    )(page_table, qs, kn, vn, zs, u, cache_k, cache_v)


def _idx_sample_kernel(pt_ref, qi_ref, wi_ref, kn_ref, *rest, ds, pp):
    kc = rest[:pp]
    oc_ref, on_ref = rest[pp:]
    p = pl.program_id(1)

    def scores(r):
        r = jnp.maximum(r, 0.0) * wi_ref[...]
        sc = _head_rows(r, ds, 0)
        for h in range(1, IDX_HEADS):
            sc = sc + _head_rows(r, ds, h)
        sc = sc * IDX_SCALE
        return jnp.where(sc == 0.0, 0.0, sc)

    for k in range(pp):
        r = jnp.dot(qi_ref[...], kc[k][...].astype(_BF16), preferred_element_type=_F32)
        oc_ref[:, k * PAGE_SIZE:(k + 1) * PAGE_SIZE] = _sortable_key(scores(r))

    @pl.when(p == 0)
    def _():
        kn = _pad_rows(kn_ref[...], PAGE_SIZE).astype(_BF16)
        sc = scores(lax.dot_general(qi_ref[...], kn, _NT, preferred_element_type=_F32))
        s_row = lax.broadcasted_iota(_I32, (ds, PAGE_SIZE), 0)
        col = lax.broadcasted_iota(_I32, (ds, PAGE_SIZE), 1)
        on_ref[...] = _sortable_key(jnp.where(col <= s_row, sc, -jnp.inf))


def _idx_sample(qi_t, wi_col, ki_new, cache_ik, page_table, layer):
    db, rows, _ = qi_t.shape
    ds = rows // IDX_HEADS
    n_pages = page_table.shape[1]
    pp = _pages_per_step(n_pages, (16, 8, 4, 2, 1))
    seq = lambda r, c: pl.BlockSpec((None, r, c), lambda b, p, pt: (b, 0, 0))
    pages = [pl.BlockSpec((None, None, IDX_DIM, PAGE_SIZE),
                          lambda b, p, pt, k=k: (layer, pt[b, p * pp + k], 0, 0)) for k in range(pp)]
    return pl.pallas_call(
        functools.partial(_idx_sample_kernel, ds=ds, pp=pp),
        grid_spec=pltpu.PrefetchScalarGridSpec(
            num_scalar_prefetch=1,
            grid=(db, n_pages // pp),
            in_specs=[seq(rows, IDX_DIM), seq(rows, 1), seq(ds, IDX_DIM)] + pages,
            out_specs=[pl.BlockSpec((None, ds, pp * PAGE_SIZE), lambda b, p, pt: (b, 0, p)),
                       seq(ds, PAGE_SIZE)]),
        out_shape=[jax.ShapeDtypeStruct((db, ds, n_pages * PAGE_SIZE), _I32),
                   jax.ShapeDtypeStruct((db, ds, PAGE_SIZE), _I32)],
        compiler_params=_params("parallel", "arbitrary"),
        name="idx_scores_sample",
    )(page_table, qi_t, wi_col, ki_new, *([cache_ik] * pp))


def _select_kernel(kc_ref, kn_ref, bc_ref, bn_ref, key_ref, bias_ref, *, ch, kk):
    n_past = kc_ref.shape[1]
    key_ref[:, :n_past] = kc_ref[...]
    key_ref[:, n_past:] = kn_ref[...]
    _select_to_bias(key_ref, bias_ref, key_ref.shape[1] // ch, ch, kk)
    bc_ref[...] = bias_ref[:, :n_past]
    bn_ref[...] = bias_ref[:, n_past:]


def _select_sample(keys_cache, keys_new, kk):
    rows, n_past = keys_cache.shape
    s = n_past + keys_new.shape[1]
    ch = _largest_divisor(s, (512, 384, 256, 128))
    full = lambda c: pl.BlockSpec((rows, c), lambda i: (0, 0))
    return pl.pallas_call(
        functools.partial(_select_kernel, ch=ch, kk=kk),
        grid=(1,),
        in_specs=[full(n_past), full(PAGE_SIZE)],
        out_specs=[full(n_past), full(PAGE_SIZE)],
        out_shape=[jax.ShapeDtypeStruct((rows, n_past), _F32), jax.ShapeDtypeStruct((rows, PAGE_SIZE), _F32)],
        scratch_shapes=[pltpu.VMEM((rows, s), _I32), pltpu.VMEM((rows, s), _F32)],
        compiler_params=_params("arbitrary"),
        name="select_sample",
    )(keys_cache, keys_new)


def _dsa_sample_kernel(pt_ref, q_ref, kn_ref, vn_ref, *rest, ds, n_steps, pp):
    kc, vc = rest[:pp], rest[pp:2 * pp]
    bn_ref, bc_ref, z_ref, o_ref, m_ref, l_ref, acc_ref = rest[2 * pp:]
    p = pl.program_id(1)
    qs = q_ref[...]

    def step(key_pages, val_pages, bias, m_old, l_old, acc_old):
        s = jnp.concatenate([_stack_scores(qs, kh, ds) for kh in key_pages], axis=1)
        s = s + jnp.concatenate([bias] * DSA_HEADS, axis=0)
        m_new = jnp.maximum(m_old, jnp.max(s, axis=1, keepdims=True))
        alpha = jnp.exp2(m_old - m_new)
        pr = jnp.exp2(s - m_new)
        pv = _stack_pv(pr[:, :PAGE_SIZE], val_pages[0], ds)
        for k in range(1, len(val_pages)):
            pv = pv + _stack_pv(pr[:, k * PAGE_SIZE:(k + 1) * PAGE_SIZE], val_pages[k], ds)
        return m_new, l_old * alpha + jnp.sum(pr, axis=1, keepdims=True), acc_old * alpha + pv

    @pl.when(p == 0)
    def _():
        rows = DSA_HEADS * ds
        m, l, acc = step([_new_heads(kn_ref, DSA_HEADS)], [_new_heads(vn_ref, DSA_HEADS)], bn_ref[...],
                         jnp.full((rows, 1), NEG_BIAS, _F32), jnp.zeros((rows, 1), _F32),
                         jnp.zeros((rows, HEAD_DIM), _F32))
        m_ref[...] = m
        l_ref[...] = l
        acc_ref[...] = acc

    m, l, acc = step([_page_heads(r) for r in kc], [_page_heads(r) for r in vc], bc_ref[...],
                     m_ref[...], l_ref[...], acc_ref[...])
    m_ref[...] = m
    l_ref[...] = l
    acc_ref[...] = acc

    @pl.when(p == n_steps - 1)
    def _():
        o_ref[...] = (acc / l * _silu(z_ref[...])).astype(o_ref.dtype)


def _dsa_sample(qs, kn, vn, cache_k, cache_v, bias_cache, bias_new, zs, page_table, layer):
    db, rows, _ = qs.shape
    ds = rows // DSA_HEADS
    n_pages = page_table.shape[1]
    pp = _pages_per_step(n_pages, (8, 4, 2, 1))
    n_steps = n_pages // pp
    w = DSA_WIDTH
    seq = lambda r, c: pl.BlockSpec((None, r, c), lambda b, p, pt: (b, 0, 0))
    pages = [pl.BlockSpec((None, None, DSA_HEADS, PAGE_SIZE, HEAD_DIM),
                          lambda b, p, pt, k=k: (layer, pt[b, p * pp + k], 0, 0, 0)) for k in range(pp)]
    return pl.pallas_call(
        functools.partial(_dsa_sample_kernel, ds=ds, n_steps=n_steps, pp=pp),
        grid_spec=pltpu.PrefetchScalarGridSpec(
            num_scalar_prefetch=1,
            grid=(db, n_steps),
            in_specs=[seq(rows, HEAD_DIM), seq(ds, w), seq(ds, w)] + pages + pages
                     + [seq(ds, PAGE_SIZE),
                        pl.BlockSpec((None, ds, pp * PAGE_SIZE), lambda b, p, pt: (b, 0, p)),
                        seq(rows, HEAD_DIM)],
            out_specs=seq(rows, HEAD_DIM),
            scratch_shapes=[pltpu.VMEM((rows, 1), _F32), pltpu.VMEM((rows, 1), _F32),
                            pltpu.VMEM((rows, HEAD_DIM), _F32)]),
        out_shape=jax.ShapeDtypeStruct(qs.shape, _BF16),
        compiler_params=_params("parallel", "arbitrary"),
        name="dsa_sample",
    )(page_table, qs, kn, vn, *([cache_k] * pp), *([cache_v] * pp), bias_new, bias_cache, zs)


def _rope_tables(pos, reps):
    def base(half):
        inv_freq = jnp.power(jnp.float32(ROPE_THETA), -jnp.arange(half, dtype=_F32) / half)
        ang = pos.astype(_F32)[:, None] * inv_freq[None, :]
        return jnp.cos(ang), jnp.sin(ang)

    tile = lambda a: jnp.tile(a, (reps, 1))
    c, s = base(HEAD_DIM // 2)
    head = (tile(jnp.concatenate([c, c], axis=1)), tile(jnp.concatenate([-s, s], axis=1)))
    c, s = base(IDX_DIM // 2)
    z = jnp.zeros_like(s)
    c64, lo64, hi64 = (jnp.concatenate([c, c], axis=1), jnp.concatenate([-s, z], axis=1),
                       jnp.concatenate([z, s], axis=1))
    idx = tuple(tile(jnp.concatenate([a, a], axis=1)) for a in (c64, lo64, hi64))
    one, zero = jnp.ones_like(c64), jnp.zeros_like(c64)
    kiwi = (tile(jnp.concatenate([c64, one], axis=1)), tile(jnp.concatenate([lo64, zero], axis=1)),
            tile(jnp.concatenate([hi64, zero], axis=1)))
    return head, idx, kiwi


_ROPE128 = (LANES // 2,)
_ROPE64 = (LANES - IDX_DIM // 2, IDX_DIM // 2)

_OFF = dict(q_sb=0, k_sb=768, v_sb=1536, z_sb=2304, q_d=3072, k_d=3840, v_d=4608, z_d=5376, qi=6144)
_KIWI0 = 7168
_TAIL0 = _KIWI0 + IDX_DIM + IDX_HEADS


def _project_all(h, w_main, w_kiwi, w_tail, layer, tabs, d_model, depth, st, kv_dtype, seq_len):
    head_t, idx_t, kiwi_t = tabs
    pj = functools.partial(_proj, h)
    kv = lambda name: [("heads_major", st[name], depth, seq_len), ("flat", kv_dtype)]
    o = {}
    (o["q_sb"],) = pj(w_main, layer, _OFF["q_sb"], SB_WIDTH, [("flat", _BF16)], name="proj_q_sb")
    st["k_sb"], o["k_sb"] = pj(w_main, layer, _OFF["k_sb"], SB_WIDTH, kv("k_sb"), name="proj_k_sb")
    st["v_sb"], o["v_sb"] = pj(w_main, layer, _OFF["v_sb"], SB_WIDTH, kv("v_sb"), name="proj_v_sb")
    (o["z_sb"],) = pj(w_main, layer, _OFF["z_sb"], SB_WIDTH, [("flat", _F32)], name="proj_z_sb")
    (o["q_d"],) = pj(w_main, layer, _OFF["q_d"], DSA_WIDTH, [("flat", _BF16)], rope=(_ROPE128, head_t),
                     name="proj_q_d")
    st["k_d"], o["k_d"] = pj(w_main, layer, _OFF["k_d"], DSA_WIDTH, kv("k_d"), rope=(_ROPE128, head_t),
                             name="proj_k_d")
    vd_req = [("heads_major", st["v_d"], depth, seq_len),
              ("flat_ones", _BF16) if kv_dtype == _BF16 else ("flat", kv_dtype)]
    st["v_d"], o["v_d"] = pj(w_main, layer, _OFF["v_d"], DSA_WIDTH, vd_req, name="proj_v_d")
    (o["z_d"],) = pj(w_main, layer, _OFF["z_d"], DSA_WIDTH, [("flat", _F32)], name="proj_z_d")
    (o["qi"],) = pj(w_main, layer, _OFF["qi"], IDX_WIDTH, [("flat", _BF16)], rope=(_ROPE64, idx_t), name="proj_qi")
    o["kiwi"], o["ki_b"], st["ki"] = pj(
        w_kiwi, layer, 0, LANES,
        [("flat", _F32), ("lanes", IDX_DIM, _BF16), ("lanes_stacked", IDX_DIM, st["ki"], depth)],
        rope=(_ROPE64, kiwi_t), name="proj_kiwi")
    (o["q_m"],) = pj(w_tail, layer, 0, MEM_WIDTH, [("flat", _F32)], name="proj_q_m")
    (o["z_m"],) = pj(w_tail, layer, MEM_WIDTH, MEM_WIDTH, [("flat", _F32)], name="proj_z_m")
    (o["gates"],) = pj(w_tail, layer, 2 * MEM_WIDTH, 3 * d_model, [("flat", _F32)], name="proj_gates")
    return o


def kernel(x_prompt, x_sample, cache_sb_k, cache_sb_v, cache_dsa_k, cache_dsa_v, cache_idx_k, cache_mem_k,
           cache_mem_v, page_table, mem_prompt, norm_g, w_in, w_up_sb, w_up_dsa, w_up_mem, w_out, w_mem_k,
           w_mem_v, final_norm_g):
    b, t, d = x_prompt.shape
    db, ds, _ = x_sample.shape
    depth = w_in.shape[0]
    n_pool = cache_sb_k.shape[1]
    n_pages = page_table.shape[1]
    n_past = n_pages * PAGE_SIZE
    mem_len = mem_prompt.shape[1]
    topk_p = min(TOPK_MAX, t // 4)
    topk_s = min(TOPK_MAX, (n_past + ds) // 4)
    assert t % Q_BLOCK == 0 and w_in.shape[2] == _TAIL0 + 2 * MEM_WIDTH + 3 * d

    w_main = w_in[:, :, :_KIWI0].astype(_BF16)
    w_kiwi = jnp.pad(w_in[:, :, _KIWI0:_TAIL0], ((0, 0), (0, 0), (0, LANES - (_TAIL0 - _KIWI0)))).astype(_BF16)
    w_tail = w_in[:, :, _TAIL0:].astype(_BF16)
    w_up_sb_b, w_up_dsa_b, w_up_mem_b = (w.astype(_BF16) for w in (w_up_sb, w_up_dsa, w_up_mem))
    w_out_b = w_out.astype(_BF16)
    w_mem_k_b, w_mem_v_b = w_mem_k.astype(_BF16), w_mem_v.astype(_BF16)
    gains = jnp.concatenate([norm_g, final_norm_g[None]], axis=0)

    tabs_p = _rope_tables(jnp.arange(t), b)
    tabs_s = _rope_tables(n_past + jnp.arange(ds), db)

    n = b * t
    x = x_prompt.reshape(n, d)
    h = _rmsnorm(x, gains[0], _BF16)
    mem_b = mem_prompt.reshape(b * mem_len, d).astype(_BF16)
    st_p = dict.fromkeys(("k_sb", "v_sb", "k_d", "v_d", "ki", "mk", "mv"))
    for l in range(depth):
        o = _project_all(h, w_main, w_kiwi, w_tail, l, tabs_p, d, depth, st_p, _BF16, t)
        g_sb = _sb_prompt(o["q_sb"], o["k_sb"], o["v_sb"], o["z_sb"], b, t)
        bias = _idx_prompt(o["qi"], o["kiwi"], o["ki_b"], b, t, topk_p)
        g_d = _dsa_prompt(o["q_d"], o["k_d"], o["v_d"], bias, o["z_d"], b, t)
        st_p["mk"], mk = _proj(mem_b, w_mem_k_b, l, 0, MEM_WIDTH, [("heads", st_p["mk"], depth), ("flat", _BF16)],
                               name="proj_mem_k")
        st_p["mv"], mv = _proj(mem_b, w_mem_v_b, l, 0, MEM_WIDTH, [("heads", st_p["mv"], depth), ("flat", _BF16)],
                               name="proj_mem_v")
        g_m = _mem_attend(o["q_m"].reshape(b, t, MEM_WIDTH), mk.reshape(b, mem_len, MEM_WIDTH),
                          mv.reshape(b, mem_len, MEM_WIDTH), o["z_m"].reshape(b, t, MEM_WIDTH))
        mixed = _mix(g_sb, g_d, g_m.reshape(n, MEM_WIDTH), o["gates"], w_up_sb_b, w_up_dsa_b, w_up_mem_b, l)
        last = l == depth - 1
        x, h = _out_proj(x, mixed, w_out_b, l, gains[l + 1], _F32 if last else _BF16)
    y_prompt = h.reshape(b, t, d)

    ns = db * ds
    x = x_sample.reshape(ns, d)
    h = _rmsnorm(x, gains[0], _BF16)

    def stack(a, heads):
        c = a.shape[1] // heads
        return a.reshape(db, ds, heads, c).transpose(0, 2, 1, 3).reshape(db, heads * ds, c)

    def unstack(a, heads):
        c = a.shape[2]
        return a.reshape(db, heads, ds, c).transpose(0, 2, 1, 3).reshape(db * ds, heads * c)

    by_head = lambda c: jnp.transpose(c, (0, 1, 3, 2, 4))
    c_sb_k, c_sb_v, c_d_k, c_d_v = (by_head(c) for c in (cache_sb_k, cache_sb_v, cache_dsa_k, cache_dsa_v))
    c_idx = jnp.transpose(cache_idx_k, (0, 1, 3, 2))

    st_s = dict.fromkeys(("k_sb", "v_sb", "k_d", "v_d", "ki"))
    for l in range(depth):
        o = _project_all(h, w_main, w_kiwi, w_tail, l, tabs_s, d, depth, st_s, _F32, ds)
        ki = o["kiwi"][:, :IDX_DIM]
        wi = o["kiwi"][:, IDX_DIM:IDX_DIM + IDX_HEADS]
        g_sb = _sb_sample(stack(o["q_sb"], SB_HEADS), o["k_sb"].reshape(db, ds, SB_WIDTH),
                          o["v_sb"].reshape(db, ds, SB_WIDTH), c_sb_k, c_sb_v,
                          stack(o["z_sb"], SB_HEADS), page_table, l)
        keys_c, keys_n = _idx_sample(stack(o["qi"], IDX_HEADS), stack(wi, IDX_HEADS),
                                     ki.reshape(db, ds, IDX_DIM), c_idx, page_table, l)
        bias_c, bias_n = _select_sample(keys_c.reshape(ns, n_past), keys_n.reshape(ns, PAGE_SIZE), topk_s)
        g_d = _dsa_sample(stack(o["q_d"], DSA_HEADS), o["k_d"].reshape(db, ds, DSA_WIDTH),
                          o["v_d"].reshape(db, ds, DSA_WIDTH), c_d_k, c_d_v,
                          bias_c.reshape(db, ds, n_past), bias_n.reshape(db, ds, PAGE_SIZE),
                          stack(o["z_d"], DSA_HEADS), page_table, l)
        g_m = _mem_attend(o["q_m"].reshape(db, ds, MEM_WIDTH), cache_mem_k, cache_mem_v,
                          o["z_m"].reshape(db, ds, MEM_WIDTH), layer=l)
        mixed = _mix(unstack(g_sb, SB_HEADS), unstack(g_d, DSA_HEADS), g_m.reshape(ns, MEM_WIDTH), o["gates"],
                     w_up_sb_b, w_up_dsa_b, w_up_mem_b, l)
        last = l == depth - 1
        x, h = _out_proj(x, mixed, w_out_b, l, gains[l + 1], _F32 if last else _BF16)
    y_sample = h.reshape(db, ds, d)

    kv = lambda a: jnp.transpose(a, (0, 1, 3, 2, 4))
    hp = lambda a: a.reshape((depth, b, -1) + a.shape[2:])
    hs = lambda a: a.reshape((depth, db, ds) + a.shape[2:])
    return (y_prompt, y_sample,
            kv(st_p["k_sb"]), kv(st_p["v_sb"]), kv(st_p["k_d"]), kv(st_p["v_d"]), hp(st_p["ki"]),
            hp(st_p["mk"]), hp(st_p["mv"]),
            kv(st_s["k_sb"]), kv(st_s["v_sb"]), kv(st_s["k_d"]), kv(st_s["v_d"]), hs(st_s["ki"]))
```

```python
import functools
import math

import numpy as np
import jax
import jax.numpy as jnp
from jax import lax
from jax.experimental import pallas as pl
from jax.experimental.pallas import tpu as pltpu

HEAD_DIM = 128
SB_HEADS = 6
DSA_HEADS = 6
MEM_HEADS = 4
SB_WIDTH = SB_HEADS * HEAD_DIM
DSA_WIDTH = DSA_HEADS * HEAD_DIM
MEM_WIDTH = MEM_HEADS * HEAD_DIM
IDX_HEADS = 16
IDX_DIM = 64
IDX_WIDTH = IDX_HEADS * IDX_DIM
PAGE_SIZE = 128
TOPK_MAX = 256
Q_BLOCK = 128
ROPE_THETA = 10000.0
NORM_EPS = 1e-6
ATTN_SCALE = HEAD_DIM ** -0.5
LOG2E = math.log2(math.e)
ATTN_SCALE2 = ATTN_SCALE * LOG2E
IDX_SCALE = (IDX_HEADS * IDX_DIM) ** -0.5

SB_DEAD_LOG2 = -160.0
LANES = 128
NEG_BIAS = -1e30
INT32_MIN = -2 ** 31
KEY_NEG_INF = -2139095041
VMEM_LIMIT = 56 * 2 ** 20

_F32 = jnp.float32
_BF16 = jnp.bfloat16
_I32 = jnp.int32
_NT = (((1,), (1,)), ((), ()))


def _params(*sem):
    return pltpu.CompilerParams(dimension_semantics=sem, vmem_limit_bytes=VMEM_LIMIT)


def _sigmoid(x):
    return 1.0 / (1.0 + jnp.exp(-x))


def _silu(x):
    return x * _sigmoid(x)


def _softplus2(t):
    return jnp.maximum(t, 0.0) + jnp.log2(1.0 + jnp.exp2(-jnp.abs(t)))


def _split_bf16(x):
    hi = x.astype(_BF16)
    lo = (x - hi.astype(_F32)).astype(_BF16)
    return hi, lo


def _sortable_key(s):
    b = lax.bitcast_convert_type(s, _I32)
    return b ^ ((b >> 31) & 0x7FFFFFFF)


def _largest_divisor(n, candidates):
    for c in candidates:
        if n % c == 0:
            return c
    raise ValueError(f"no tile in {candidates} divides {n}")


def _rms_kernel(x_ref, g_ref, o_ref):
    x = x_ref[...]
    y = x * lax.rsqrt(jnp.mean(x * x, axis=-1, keepdims=True) + NORM_EPS)
    o_ref[...] = (y * g_ref[...]).astype(o_ref.dtype)


def _rmsnorm(x, g, out_dtype):
    n, d = x.shape
    tm = _largest_divisor(n, (512, 256, 128, 64))
    return pl.pallas_call(
        _rms_kernel,
        grid=(n // tm,),
        in_specs=[pl.BlockSpec((tm, d), lambda i: (i, 0)),
                  pl.BlockSpec((1, d), lambda i: (0, 0))],
        out_specs=pl.BlockSpec((tm, d), lambda i: (i, 0)),
        out_shape=jax.ShapeDtypeStruct((n, d), out_dtype),
        compiler_params=_params("parallel"),
        name="rmsnorm",
    )(x, g.reshape(1, d))


def _proj_kernel(*refs, shifts, kinds, n_alias):
    h_ref, w_ref = refs[0], refs[1]
    n_tab = 1 + len(shifts) if shifts else 0
    tabs = refs[2:2 + n_tab]
    outs = refs[2 + n_tab + n_alias:]
    z = jnp.dot(h_ref[...], w_ref[...], preferred_element_type=_F32)
    tn = z.shape[1]
    for g in range(tn // LANES):
        zg = z[:, g * LANES:(g + 1) * LANES]
        if shifts:
            r = zg * tabs[0][...]
            for t, sh in enumerate(shifts):
                r = r + pltpu.roll(zg, sh, 1) * tabs[1 + t][...]
            zg = r
        for kind, o_ref in zip(kinds, outs):
            if kind == "flat":
                o_ref[:, g * LANES:(g + 1) * LANES] = zg.astype(o_ref.dtype)
            elif kind == "flat_ones":
                o_ref[:, 2 * g * LANES:(2 * g + 1) * LANES] = zg.astype(o_ref.dtype)
                o_ref[:, (2 * g + 1) * LANES:(2 * g + 2) * LANES] = jnp.ones(zg.shape, o_ref.dtype)
            elif kind == "heads":
                o_ref[:, g, :] = zg
            elif kind == "heads_major":
                if len(o_ref.shape) == 3:
                    o_ref[g] = zg
                else:
                    gr = o_ref.shape[2]
                    for b in range(o_ref.shape[0]):
                        o_ref[b, g] = zg[b * gr:(b + 1) * gr]
            else:
                o_ref[...] = zg[:, :o_ref.shape[1]].astype(o_ref.dtype)


def _proj(h, w, layer, col0, width, outs, rope=None, name="proj"):
    n, d = h.shape
    tm = _largest_divisor(n, (1024, 512, 256, 128, 64))
    tn = _largest_divisor(width, (1024, 768, 512, 256, 128))
    assert col0 % tn == 0
    cb = col0 // tn
    shifts, tables = rope if rope else ((), ())
    in_specs = [pl.BlockSpec((tm, d), lambda j, i: (i, 0)),
                pl.BlockSpec((None, d, tn), lambda j, i: (layer, 0, cb + j))]
    operands = [h, w]
    for t in tables:
        assert t.shape == (n, LANES)
        in_specs.append(pl.BlockSpec((tm, LANES), lambda j, i: (i, 0)))
        operands.append(t)
    kinds, out_specs, out_shape, aliases = [], [], [], {}
    prevs = []
    for req in outs:
        kind = req[0]
        if kind == "flat":
            out_specs.append(pl.BlockSpec((tm, tn), lambda j, i: (i, j)))
            out_shape.append(jax.ShapeDtypeStruct((n, width), req[1]))
        elif kind == "flat_ones":
            out_specs.append(pl.BlockSpec((tm, 2 * tn), lambda j, i: (i, j)))
            out_shape.append(jax.ShapeDtypeStruct((n, 2 * width), req[1]))
        elif kind == "lanes":
            assert width == LANES
            out_specs.append(pl.BlockSpec((tm, req[1]), lambda j, i: (i, 0)))
            out_shape.append(jax.ShapeDtypeStruct((n, req[1]), req[2]))
        elif kind == "heads":
            assert tn == width
            nh = width // LANES
            out_specs.append(pl.BlockSpec((None, tm, nh, LANES), lambda j, i: (layer, i, 0, 0)))
            out_shape.append(jax.ShapeDtypeStruct((req[2], n, nh, LANES), _F32))
            prevs.append((len(kinds), req[1]))
        elif kind == "heads_major":
            assert tn == width
            nh = width // LANES
            gr = req[3]
            if tm <= gr:
                per = gr // tm
                out_specs.append(pl.BlockSpec((None, None, nh, tm, LANES),
                                              lambda j, i, per=per: (layer, i // per, 0, i % per, 0)))
            else:
                out_specs.append(pl.BlockSpec((None, tm // gr, nh, gr, LANES), lambda j, i: (layer, i, 0, 0, 0)))
            out_shape.append(jax.ShapeDtypeStruct((req[2], n // gr, nh, gr, LANES), _F32))
            prevs.append((len(kinds), req[1]))
        elif kind == "lanes_stacked":
            assert width == LANES
            out_specs.append(pl.BlockSpec((None, tm, req[1]), lambda j, i: (layer, i, 0)))
            out_shape.append(jax.ShapeDtypeStruct((req[3], n, req[1]), _F32))
            prevs.append((len(kinds), req[2]))
        else:
            raise ValueError(kind)
        kinds.append(kind)
    n_alias = 0
    for out_idx, prev in prevs:
        if prev is not None:
            aliases[len(operands)] = out_idx
            in_specs.append(pl.BlockSpec(memory_space=pl.ANY))
            operands.append(prev)
            n_alias += 1
    return pl.pallas_call(
        functools.partial(_proj_kernel, shifts=tuple(shifts), kinds=tuple(kinds), n_alias=n_alias),
        grid=(width // tn, n // tm),
        in_specs=in_specs,
        out_specs=out_specs,
        out_shape=out_shape,
        input_output_aliases=aliases,
        compiler_params=_params("parallel", "parallel"),
        name=name,
    )(*operands)


def _cumsum_matrix(t):
    r = np.arange(2 * t) % t
    c = np.arange(2 * t)
    m = np.where(c[None, :] < t, r[:, None] > c[None, :], True)
    return jnp.asarray(m, dtype=_BF16)


def _sb_prompt_kernel(q_ref, k_ref, v_ref, z_ref, u_ref, o_ref, carry_ref, acc_ref, *, tq):
    i = pl.program_id(1)
    heads = SB_HEADS
    hsl = [slice(h * HEAD_DIM, (h + 1) * HEAD_DIM) for h in range(heads)]
    rsl = [slice(h * tq, (h + 1) * tq) for h in range(heads)]

    def key_tiles(off, nt, before):
        n = heads * tq
        s = jnp.concatenate(
            [lax.dot_general(q_ref[:, hsl[h]], k_ref[pl.ds(off + (nt - 1 - j) * tq, tq), hsl[h]], _NT,
                             preferred_element_type=_F32)
             for j in range(nt) for h in range(heads)], axis=0) * ATTN_SCALE2
        sp = _softplus2(s)
        lk = -sp
        if before is not None:
            lk = jnp.where(before, lk, 0.0)
        hi, lo = _split_bf16(lk)
        cr = jnp.dot(jnp.concatenate([hi, lo], axis=1), u_ref[...], preferred_element_type=_F32)
        carry = carry_ref[...]
        for j in range(nt):
            r = slice(j * n, (j + 1) * n)
            a = jnp.exp2(s[r] - sp[r] + carry + cr[r, :tq])
            if before is not None:
                a = jnp.where(before, a, 0.0)
            carry = carry + cr[r, tq:]
            ab = a.astype(_BF16)
            for h in range(heads):
                acc_ref[rsl[h], :] += jnp.dot(ab[rsl[h]], v_ref[pl.ds(off + (nt - 1 - j) * tq, tq), hsl[h]],
                                              preferred_element_type=_F32)
        carry_ref[...] = carry

    carry_ref[...] = jnp.zeros_like(carry_ref)
    acc_ref[...] = jnp.zeros_like(acc_ref)
    row = lax.broadcasted_iota(_I32, (heads * tq, tq), 0) & (tq - 1)
    col = lax.broadcasted_iota(_I32, (heads * tq, tq), 1)
    key_tiles(pl.multiple_of(i * tq, tq), 1, col < row)

    def live():
        return jnp.max(carry_ref[...]) > SB_DEAD_LOG2

    def body(c):
        key_tiles(pl.multiple_of((i - 2 - 2 * c[0]) * tq, tq), 2, None)
        return c[0] + 1, live()

    _, alive = lax.while_loop(lambda c: jnp.logical_and(c[0] < i // 2, c[1]), body, (jnp.int32(0), live()))

    @pl.when(jnp.logical_and(i % 2 == 1, alive))
    def _():
        key_tiles(0, 1, None)
    for h in range(heads):
        o_ref[:, hsl[h]] = (acc_ref[rsl[h], :] * _silu(z_ref[:, hsl[h]])).astype(o_ref.dtype)


def _sb_prompt(q, k, v, z, b, t):
    tq = Q_BLOCK
    nq = t // tq
    w = SB_WIDTH
    u = _cumsum_matrix(tq)
    return pl.pallas_call(
        functools.partial(_sb_prompt_kernel, tq=tq),
        grid=(b, nq),
        in_specs=[pl.BlockSpec((tq, w), lambda bi, i: (bi * nq + i, 0)),
                  pl.BlockSpec((t, w), lambda bi, i: (bi, 0)),
                  pl.BlockSpec((t, w), lambda bi, i: (bi, 0)),
                  pl.BlockSpec((tq, w), lambda bi, i: (bi * nq + i, 0)),
                  pl.BlockSpec((2 * tq, 2 * tq), lambda bi, i: (0, 0))],
        out_specs=pl.BlockSpec((tq, w), lambda bi, i: (bi * nq + i, 0)),
        out_shape=jax.ShapeDtypeStruct(q.shape, _BF16),
        scratch_shapes=[pltpu.VMEM((SB_HEADS * tq, tq), _F32), pltpu.VMEM((SB_HEADS * tq, HEAD_DIM), _F32)],
        compiler_params=_params("parallel", "arbitrary"),
        name="sb_prompt",
    )(q, k, v, z, u)


def _select_to_bias(key_ref, bias_ref, nch, ch, kk):
    rows = key_ref.shape[0]

    def count(pred):
        def body(c, part):
            off = pl.multiple_of(c * ch, ch)
            m = jnp.where(pred(key_ref[:, pl.ds(off, ch)], off), 1.0, 0.0).astype(_F32)
            for g in range(ch // LANES):
                part = part + m[:, g * LANES:(g + 1) * LANES]
            return part
        part = lax.fori_loop(0, nch, body, jnp.zeros((rows, LANES), _F32))
        return jnp.sum(part, axis=1, keepdims=True).astype(_I32)

    def bit_body(it, u):
        cand_u = u | lax.shift_left(jnp.int32(1), 31 - it)
        cand = cand_u ^ jnp.int32(INT32_MIN)
        cnt = count(lambda kc, off: kc >= cand)
        return jnp.where(cnt >= kk, cand_u, u)

    u = lax.fori_loop(0, 32, bit_body, jnp.zeros((rows, 1), _I32))
    thr = u ^ jnp.int32(INT32_MIN)
    cnt_gt = count(lambda kc, off: kc > thr)
    cnt_ge = count(lambda kc, off: kc >= thr)
    need = kk - cnt_gt
    excess = jnp.where((cnt_ge > kk) & (thr > KEY_NEG_INF), 1, 0).astype(_I32)
    nbits = max(1, int(math.ceil(math.log2(key_ref.shape[1] + 1))))

    def tie_cut(_):
        def q_body(it, q):
            cand = q | lax.shift_left(jnp.int32(1), nbits - 1 - it)

            def pred(kc, off):
                idx = off + lax.broadcasted_iota(_I32, kc.shape, 1)
                return (kc == thr) & (idx < cand)
            return jnp.where(count(pred) < need, cand, q)
        return lax.fori_loop(0, nbits, q_body, jnp.zeros((rows, 1), _I32))

    def no_cut(_):
        return jnp.full((rows, 1), 2 ** 30, _I32)

    qcut = lax.cond(jnp.max(excess) > 0, tie_cut, no_cut, 0)

    def write(c, _):
        off = pl.multiple_of(c * ch, ch)
        kc = key_ref[:, pl.ds(off, ch)]
        idx = off + lax.broadcasted_iota(_I32, kc.shape, 1)
        sel = (kc > thr) | ((kc == thr) & (idx <= qcut))
        ok = sel & (kc > KEY_NEG_INF)
        bias_ref[:, pl.ds(off, ch)] = jnp.where(ok, 0.0, NEG_BIAS).astype(_F32)
        return 0

    lax.fori_loop(0, nch, write, 0)


def _idx_prompt_kernel(qi_ref, wi_ref, ki_ref, bias_ref, key_ref, *, tq, ch, kk):
    i = pl.program_id(1)
    nch = (i * tq + tq + ch - 1) // ch
    qi = qi_ref[...]
    wi = wi_ref[...]
    rowpos = i * tq + lax.broadcasted_iota(_I32, (tq, ch), 0)

    def score(c, _):
        off = pl.multiple_of(c * ch, ch)
        kc = ki_ref[pl.ds(off, ch), :]
        acc = jnp.zeros((tq, ch), _F32)
        for h in range(IDX_HEADS):
            r = lax.dot_general(qi[:, h * IDX_DIM:(h + 1) * IDX_DIM], kc, _NT,
                                preferred_element_type=_F32)
            acc = acc + jnp.maximum(r, 0.0) * wi[:, IDX_DIM + h:IDX_DIM + h + 1]
        sc = acc * IDX_SCALE
        sc = jnp.where(sc == 0.0, 0.0, sc)
        colpos = off + lax.broadcasted_iota(_I32, (tq, ch), 1)
        sc = jnp.where(colpos <= rowpos, sc, -jnp.inf)
        key_ref[:, pl.ds(off, ch)] = _sortable_key(sc)
        return 0

    lax.fori_loop(0, nch, score, 0)
    bias_ref[...] = jnp.full(bias_ref.shape, NEG_BIAS, _F32)
    _select_to_bias(key_ref, bias_ref, nch, ch, kk)


def _idx_prompt(qi, kiwi, ki, b, t, kk):
    tq = Q_BLOCK
    nq = t // tq
    ch = _largest_divisor(t, (512, 256, 128))
    return pl.pallas_call(
        functools.partial(_idx_prompt_kernel, tq=tq, ch=ch, kk=kk),
        grid=(b, nq),
        in_specs=[pl.BlockSpec((tq, IDX_WIDTH), lambda bi, i: (bi * nq + i, 0)),
                  pl.BlockSpec((tq, LANES), lambda bi, i: (bi * nq + i, 0)),
                  pl.BlockSpec((t, IDX_DIM), lambda bi, i: (bi, 0))],
        out_specs=pl.BlockSpec((tq, t), lambda bi, i: (bi * nq + i, 0)),
        out_shape=jax.ShapeDtypeStruct((b * t, t), _F32),
        scratch_shapes=[pltpu.VMEM((tq, t), _I32)],
        compiler_params=_params("parallel", "arbitrary"),
        name="idx_select_prompt",
    )(qi, kiwi, ki)


def _dsa_prompt_kernel(q_ref, k_ref, v_ref, bias_ref, z_ref, o_ref, m_ref, l_ref, acc_ref, *, tq, ch):
    i = pl.program_id(1)
    nch = (i * tq + tq + ch - 1) // ch
    heads = DSA_HEADS
    hsl = [slice(h * HEAD_DIM, (h + 1) * HEAD_DIM) for h in range(heads)]
    rsl = [slice(h * tq, (h + 1) * tq) for h in range(heads)]
    m_ref[...] = jnp.full(m_ref.shape, NEG_BIAS, _F32)
    l_ref[...] = jnp.zeros_like(l_ref)
    acc_ref[...] = jnp.zeros_like(acc_ref)

    def body(c, _):
        off = pl.multiple_of(c * ch, ch)
        bias = bias_ref[:, pl.ds(off, ch)]
        s = jnp.concatenate(
            [lax.dot_general(q_ref[:, hsl[h]], k_ref[pl.ds(off, ch), hsl[h]], _NT, preferred_element_type=_F32)
             for h in range(heads)], axis=0) * (ATTN_SCALE * LOG2E) + jnp.concatenate([bias] * heads, axis=0)
        m_old = m_ref[...]
        m_new = jnp.maximum(m_old, jnp.max(s, axis=1, keepdims=True))
        alpha = jnp.exp2(m_old - m_new)
        pb = jnp.exp2(s - m_new).astype(_BF16)
        for h in range(heads):
            pv = jnp.dot(pb[rsl[h]], v_ref[pl.ds(off, ch), 2 * h * HEAD_DIM:(2 * h + 2) * HEAD_DIM],
                         preferred_element_type=_F32)
            acc_ref[rsl[h], :] = acc_ref[rsl[h], :] * alpha[rsl[h]] + pv[:, :HEAD_DIM]
            l_ref[rsl[h], :] = l_ref[rsl[h], :] * alpha[rsl[h]] + pv[:, HEAD_DIM:]
        m_ref[...] = m_new
        return 0

    lax.fori_loop(0, nch, body, 0)
    for h in range(heads):
        o_ref[:, hsl[h]] = (acc_ref[rsl[h], :] / l_ref[rsl[h], :] * _silu(z_ref[:, hsl[h]])).astype(o_ref.dtype)


def _dsa_prompt(q, k, v, bias, z, b, t):
    tq = Q_BLOCK
    nq = t // tq
    ch = _largest_divisor(t, (1024, 512, 256, 128))
    w = DSA_WIDTH
    return pl.pallas_call(
        functools.partial(_dsa_prompt_kernel, tq=tq, ch=ch),
        grid=(b, nq),
        in_specs=[pl.BlockSpec((tq, w), lambda bi, i: (bi * nq + i, 0)),
                  pl.BlockSpec((t, w), lambda bi, i: (bi, 0)),
                  pl.BlockSpec((t, 2 * w), lambda bi, i: (bi, 0)),
                  pl.BlockSpec((tq, t), lambda bi, i: (bi * nq + i, 0)),
                  pl.BlockSpec((tq, w), lambda bi, i: (bi * nq + i, 0))],
        out_specs=pl.BlockSpec((tq, w), lambda bi, i: (bi * nq + i, 0)),
        out_shape=jax.ShapeDtypeStruct(q.shape, _BF16),
        scratch_shapes=[pltpu.VMEM((DSA_HEADS * tq, 1), _F32), pltpu.VMEM((DSA_HEADS * tq, LANES), _F32),
                        pltpu.VMEM((DSA_HEADS * tq, HEAD_DIM), _F32)],
        compiler_params=_params("parallel", "arbitrary"),
        name="dsa_prompt",
    )(q, k, v, bias, z)


def _mem_kernel(q_ref, mk_ref, mv_ref, z_ref, o_ref):
    tm = q_ref.shape[0]
    pad = (-tm) % 16
    by_head = len(mk_ref.shape) == 3
    for h in range(MEM_HEADS):
        hs = slice(h * HEAD_DIM, (h + 1) * HEAD_DIM)
        mk = mk_ref[:, h, :] if by_head else mk_ref[:, hs]
        mv = mv_ref[:, h, :] if by_head else mv_ref[:, hs]
        qh = q_ref[:, hs].astype(_F32)
        if pad:
            qh = jnp.concatenate([qh, jnp.zeros((pad, HEAD_DIM), _F32)], axis=0)
        s = lax.dot_general(qh.astype(_BF16), mk.astype(_BF16), _NT, preferred_element_type=_F32) * ATTN_SCALE2
        p = jnp.exp2(s - jnp.max(s, axis=1, keepdims=True))
        p = p / jnp.sum(p, axis=1, keepdims=True)
        o = jnp.dot(p.astype(_BF16), mv.astype(_BF16), preferred_element_type=_F32)
        o_ref[:, hs] = (o[:tm] * _silu(z_ref[:, hs])).astype(o_ref.dtype)


def _mem_attend(q, mk, mv, z, layer=None):
    g, r, w = q.shape
    tm = _largest_divisor(r, (512, 256, 128, 64, 8))
    if layer is None:
        m = mk.shape[1]
        mem = pl.BlockSpec((None, m, w), lambda gi, i: (gi, 0, 0))
    else:
        m = mk.shape[2]
        mem = pl.BlockSpec((None, None, m, MEM_HEADS, HEAD_DIM), lambda gi, i: (layer, gi, 0, 0, 0))
    return pl.pallas_call(
        _mem_kernel,
        grid=(g, r // tm),
        in_specs=[pl.BlockSpec((None, tm, w), lambda gi, i: (gi, i, 0)),
                  mem, mem,
                  pl.BlockSpec((None, tm, w), lambda gi, i: (gi, i, 0))],
        out_specs=pl.BlockSpec((None, tm, w), lambda gi, i: (gi, i, 0)),
        out_shape=jax.ShapeDtypeStruct((g, r, w), _BF16),
        compiler_params=_params("parallel", "parallel"),
        name="mem_attend",
    )(q, mk, mv, z)


def _mix_kernel(gs_ref, gd_ref, gm_ref, g1_ref, g2_ref, g3_ref, w1_ref, w2_ref, w3_ref, o_ref):
    a = jnp.dot(gs_ref[...], w1_ref[...], preferred_element_type=_F32)
    c = jnp.dot(gd_ref[...], w2_ref[...], preferred_element_type=_F32)
    m = jnp.dot(gm_ref[...], w3_ref[...], preferred_element_type=_F32)
    mixed = _sigmoid(g1_ref[...]) * a + _sigmoid(g2_ref[...]) * c + _sigmoid(g3_ref[...]) * m
    o_ref[...] = mixed.astype(o_ref.dtype)


def _mix(gs, gd, gm, gates, w1, w2, w3, layer):
    n = gs.shape[0]
    d = w1.shape[2]
    tm = _largest_divisor(n, (256, 128, 64))
    row = lambda w: pl.BlockSpec((tm, w), lambda i: (i, 0))
    wspec = lambda w: pl.BlockSpec((None, w.shape[1], d), lambda i: (layer, 0, 0))
    return pl.pallas_call(
        _mix_kernel,
        grid=(n // tm,),
        in_specs=[row(SB_WIDTH), row(DSA_WIDTH), row(MEM_WIDTH),
                  pl.BlockSpec((tm, d), lambda i: (i, 0)),
                  pl.BlockSpec((tm, d), lambda i: (i, 1)),
                  pl.BlockSpec((tm, d), lambda i: (i, 2)),
                  wspec(w1), wspec(w2), wspec(w3)],
        out_specs=pl.BlockSpec((tm, d), lambda i: (i, 0)),
        out_shape=jax.ShapeDtypeStruct((n, d), _BF16),
        compiler_params=_params("parallel"),
        name="mix",
    )(gs, gd, gm, gates, gates, gates, w1, w2, w3)


def _out_kernel(x_ref, m_ref, w_ref, g_ref, xo_ref, ho_ref):
    xn = x_ref[...] + jnp.dot(m_ref[...], w_ref[...], preferred_element_type=_F32)
    xo_ref[...] = xn
    y = xn * lax.rsqrt(jnp.mean(xn * xn, axis=-1, keepdims=True) + NORM_EPS)
    ho_ref[...] = (y * g_ref[...]).astype(ho_ref.dtype)


def _out_proj(x, mixed, w, layer, g, h_dtype):
    n, d = x.shape
    tm = _largest_divisor(n, (512, 256, 128, 64))
    return pl.pallas_call(
        _out_kernel,
        grid=(n // tm,),
        in_specs=[pl.BlockSpec((tm, d), lambda i: (i, 0)),
                  pl.BlockSpec((tm, d), lambda i: (i, 0)),
                  pl.BlockSpec((None, d, d), lambda i: (layer, 0, 0)),
                  pl.BlockSpec((1, d), lambda i: (0, 0))],
        out_specs=[pl.BlockSpec((tm, d), lambda i: (i, 0)),
                   pl.BlockSpec((tm, d), lambda i: (i, 0))],
        out_shape=[jax.ShapeDtypeStruct((n, d), _F32), jax.ShapeDtypeStruct((n, d), h_dtype)],
        compiler_params=_params("parallel"),
        name="out_proj",
    )(x, mixed, w, g.reshape(1, d))


def _head_rows(x, ds, h):
    return x[h * ds:(h + 1) * ds]


def _stack_scores(qs, key_heads, ds):
    parts = []
    for h, kh in enumerate(key_heads):
        r = lax.dot_general(qs, kh, _NT, preferred_element_type=_F32)
        parts.append(_head_rows(r, ds, h))
    return jnp.concatenate(parts, axis=0) * ATTN_SCALE2


def _stack_pv(p, val_heads, ds):
    pb = p.astype(_BF16)
    parts = []
    for h, vh in enumerate(val_heads):
        r = jnp.dot(pb, vh, preferred_element_type=_F32)
        parts.append(_head_rows(r, ds, h))
    return jnp.concatenate(parts, axis=0)


def _pad_rows(x, rows):
    return jnp.concatenate([x, jnp.zeros((rows - x.shape[0], x.shape[1]), x.dtype)], axis=0)


def _page_heads(ref):
    return [ref[h].astype(_BF16) for h in range(ref.shape[0])]


def _new_heads(ref, heads):
    x = _pad_rows(ref[...], PAGE_SIZE).astype(_BF16)
    return [x[:, h * HEAD_DIM:(h + 1) * HEAD_DIM] for h in range(heads)]


def _pages_per_step(n_pages, candidates):
    return _largest_divisor(n_pages, candidates)


def _sb_sample_kernel(pt_ref, q_ref, kn_ref, vn_ref, z_ref, u_ref, kc_hbm, vc_hbm, o_ref,
                      kbuf, vbuf, sem, carry_ref, acc_ref, *, ds, n_pages, pp, layer):
    b = pl.program_id(0)
    n_steps = n_pages // pp
    rows = SB_HEADS * ds
    qs = q_ref[...]

    def page_copies(step, slot):
        cps = []
        for k in range(pp):
            page = pt_ref[b, n_pages - 1 - (step * pp + k)]
            cps.append(pltpu.make_async_copy(kc_hbm.at[layer, page], kbuf.at[slot, k], sem.at[slot, 0, k]))
            cps.append(pltpu.make_async_copy(vc_hbm.at[layer, page], vbuf.at[slot, k], sem.at[slot, 1, k]))
        return cps

    def start(step, slot):
        for cp in page_copies(step, slot):
            cp.start()

    def wait(step, slot):
        for cp in page_copies(step, slot):
            cp.wait()

    start(0, 0)

    def page(key_heads, val_heads, before, carry, acc):
        s = _stack_scores(qs, key_heads, ds)
        sp = _softplus2(s)
        lk = -sp
        if before is not None:
            lk = jnp.where(before, lk, 0.0)
        hi, lo = _split_bf16(lk)
        cr = jnp.dot(jnp.concatenate([hi, lo], axis=1), u_ref[...], preferred_element_type=_F32)
        a = jnp.exp2(s - sp + carry + cr[:, :PAGE_SIZE])
        if before is not None:
            a = jnp.where(before, a, 0.0)
        return carry + cr[:, PAGE_SIZE:], acc + _stack_pv(a, val_heads, ds)

    qrow = lax.broadcasted_iota(_I32, (rows, PAGE_SIZE), 0) % ds
    col = lax.broadcasted_iota(_I32, (rows, PAGE_SIZE), 1)
    carry, acc = page(_new_heads(kn_ref, SB_HEADS), _new_heads(vn_ref, SB_HEADS), col < qrow,
                      jnp.zeros((rows, PAGE_SIZE), _F32), jnp.zeros((rows, HEAD_DIM), _F32))
    carry_ref[...] = carry
    acc_ref[...] = acc

    def live():
        return jnp.max(carry_ref[...]) > SB_DEAD_LOG2

    def body(c):
        step = c[0]
        slot = step % 2

        @pl.when(step + 1 < n_steps)
        def _():
            start(step + 1, 1 - slot)

        wait(step, slot)
        carry, acc = carry_ref[...], acc_ref[...]
        s = jnp.concatenate([_stack_scores(qs, _page_heads(kbuf.at[slot, k]), ds) for k in range(pp)], axis=0)
        sp = _softplus2(s)
        hi, lo = _split_bf16(-sp)
        cr = jnp.dot(jnp.concatenate([hi, lo], axis=1), u_ref[...], preferred_element_type=_F32)
        weights = []
        for k in range(pp):
            r = slice(k * rows, (k + 1) * rows)
            weights.append(jnp.exp2(s[r] - sp[r] + carry + cr[r, :PAGE_SIZE]))
            carry = carry + cr[r, PAGE_SIZE:]
        for k in range(pp):
            acc = acc + _stack_pv(weights[k], _page_heads(vbuf.at[slot, k]), ds)
        carry_ref[...] = carry
        acc_ref[...] = acc
        return step + 1, live()

    done, _ = lax.while_loop(lambda c: jnp.logical_and(c[0] < n_steps, c[1]), body, (jnp.int32(0), live()))

    @pl.when(done < n_steps)
    def _():
        wait(done, done % 2)

    o_ref[...] = (acc_ref[...] * _silu(z_ref[...])).astype(o_ref.dtype)


def _sb_sample(qs, kn, vn, cache_k, cache_v, zs, page_table, layer):
    db, rows, _ = qs.shape
    ds = rows // SB_HEADS
    n_pages = page_table.shape[1]
    pp = _pages_per_step(n_pages, (4, 2, 1))
    w = SB_WIDTH
    u = _cumsum_matrix(PAGE_SIZE)
    seq = lambda r, c: pl.BlockSpec((None, r, c), lambda b, pt: (b, 0, 0))
    hbm = pl.BlockSpec(memory_space=pl.ANY)
    page_buf = pltpu.VMEM((2, pp, SB_HEADS, PAGE_SIZE, HEAD_DIM), _F32)
    return pl.pallas_call(
        functools.partial(_sb_sample_kernel, ds=ds, n_pages=n_pages, pp=pp, layer=layer),
        grid_spec=pltpu.PrefetchScalarGridSpec(
            num_scalar_prefetch=1,
            grid=(db,),
            in_specs=[seq(rows, HEAD_DIM), seq(ds, w), seq(ds, w), seq(rows, HEAD_DIM),
                      pl.BlockSpec((2 * PAGE_SIZE, 2 * PAGE_SIZE), lambda b, pt: (0, 0)), hbm, hbm],
            out_specs=seq(rows, HEAD_DIM),
            scratch_shapes=[page_buf, page_buf, pltpu.SemaphoreType.DMA((2, 2, pp)),
                            pltpu.VMEM((rows, PAGE_SIZE), _F32), pltpu.VMEM((rows, HEAD_DIM), _F32)]),
        out_shape=jax.ShapeDtypeStruct(qs.shape, _BF16),
        compiler_params=_params("arbitrary"),
        name="sb_sample",
    )(page_table, qs, kn, vn, zs, u, cache_k, cache_v)


def _idx_sample_kernel(pt_ref, qi_ref, wi_ref, kn_ref, *rest, ds, pp):
    kc = rest[:pp]
    oc_ref, on_ref = rest[pp:]
    p = pl.program_id(1)

    def scores(r):
        r = jnp.maximum(r, 0.0) * wi_ref[...]
        sc = _head_rows(r, ds, 0)
        for h in range(1, IDX_HEADS):
            sc = sc + _head_rows(r, ds, h)
        sc = sc * IDX_SCALE
        return jnp.where(sc == 0.0, 0.0, sc)

    for k in range(pp):
        r = jnp.dot(qi_ref[...], kc[k][...].astype(_BF16), preferred_element_type=_F32)
        oc_ref[:, k * PAGE_SIZE:(k + 1) * PAGE_SIZE] = _sortable_key(scores(r))

    @pl.when(p == 0)
    def _():
        kn = _pad_rows(kn_ref[...], PAGE_SIZE).astype(_BF16)
        sc = scores(lax.dot_general(qi_ref[...], kn, _NT, preferred_element_type=_F32))
        s_row = lax.broadcasted_iota(_I32, (ds, PAGE_SIZE), 0)
        col = lax.broadcasted_iota(_I32, (ds, PAGE_SIZE), 1)
        on_ref[...] = _sortable_key(jnp.where(col <= s_row, sc, -jnp.inf))


def _idx_sample(qi_t, wi_col, ki_new, cache_ik, page_table, layer):
    db, rows, _ = qi_t.shape
    ds = rows // IDX_HEADS
    n_pages = page_table.shape[1]
    pp = _pages_per_step(n_pages, (16, 8, 4, 2, 1))
    seq = lambda r, c: pl.BlockSpec((None, r, c), lambda b, p, pt: (b, 0, 0))
    pages = [pl.BlockSpec((None, None, IDX_DIM, PAGE_SIZE),
                          lambda b, p, pt, k=k: (layer, pt[b, p * pp + k], 0, 0)) for k in range(pp)]
    return pl.pallas_call(
        functools.partial(_idx_sample_kernel, ds=ds, pp=pp),
        grid_spec=pltpu.PrefetchScalarGridSpec(
            num_scalar_prefetch=1,
            grid=(db, n_pages // pp),
            in_specs=[seq(rows, IDX_DIM), seq(rows, 1), seq(ds, IDX_DIM)] + pages,
            out_specs=[pl.BlockSpec((None, ds, pp * PAGE_SIZE), lambda b, p, pt: (b, 0, p)),
                       seq(ds, PAGE_SIZE)]),
        out_shape=[jax.ShapeDtypeStruct((db, ds, n_pages * PAGE_SIZE), _I32),
                   jax.ShapeDtypeStruct((db, ds, PAGE_SIZE), _I32)],
        compiler_params=_params("parallel", "arbitrary"),
        name="idx_scores_sample",
    )(page_table, qi_t, wi_col, ki_new, *([cache_ik] * pp))


def _select_kernel(kc_ref, kn_ref, bc_ref, bn_ref, key_ref, bias_ref, *, ch, kk):
    n_past = kc_ref.shape[1]
    key_ref[:, :n_past] = kc_ref[...]
    key_ref[:, n_past:] = kn_ref[...]
    _select_to_bias(key_ref, bias_ref, key_ref.shape[1] // ch, ch, kk)
    bc_ref[...] = bias_ref[:, :n_past]
    bn_ref[...] = bias_ref[:, n_past:]


def _select_sample(keys_cache, keys_new, kk):
    rows, n_past = keys_cache.shape
    s = n_past + keys_new.shape[1]
    ch = _largest_divisor(s, (512, 384, 256, 128))
    full = lambda c: pl.BlockSpec((rows, c), lambda i: (0, 0))
    return pl.pallas_call(
        functools.partial(_select_kernel, ch=ch, kk=kk),
        grid=(1,),
        in_specs=[full(n_past), full(PAGE_SIZE)],
        out_specs=[full(n_past), full(PAGE_SIZE)],
        out_shape=[jax.ShapeDtypeStruct((rows, n_past), _F32), jax.ShapeDtypeStruct((rows, PAGE_SIZE), _F32)],
        scratch_shapes=[pltpu.VMEM((rows, s), _I32), pltpu.VMEM((rows, s), _F32)],
        compiler_params=_params("arbitrary"),
        name="select_sample",
    )(keys_cache, keys_new)


def _dsa_sample_kernel(pt_ref, q_ref, kn_ref, vn_ref, *rest, ds, n_steps, pp):
    kc, vc = rest[:pp], rest[pp:2 * pp]
    bn_ref, bc_ref, z_ref, o_ref, m_ref, l_ref, acc_ref = rest[2 * pp:]
    p = pl.program_id(1)
    qs = q_ref[...]

    def step(key_pages, val_pages, bias, m_old, l_old, acc_old):
        s = jnp.concatenate([_stack_scores(qs, kh, ds) for kh in key_pages], axis=1)
        s = s + jnp.concatenate([bias] * DSA_HEADS, axis=0)
        m_new = jnp.maximum(m_old, jnp.max(s, axis=1, keepdims=True))
        alpha = jnp.exp2(m_old - m_new)
        pr = jnp.exp2(s - m_new)
        pv = _stack_pv(pr[:, :PAGE_SIZE], val_pages[0], ds)
        for k in range(1, len(val_pages)):
            pv = pv + _stack_pv(pr[:, k * PAGE_SIZE:(k + 1) * PAGE_SIZE], val_pages[k], ds)
        return m_new, l_old * alpha + jnp.sum(pr, axis=1, keepdims=True), acc_old * alpha + pv

    @pl.when(p == 0)
    def _():
        rows = DSA_HEADS * ds
        m, l, acc = step([_new_heads(kn_ref, DSA_HEADS)], [_new_heads(vn_ref, DSA_HEADS)], bn_ref[...],
                         jnp.full((rows, 1), NEG_BIAS, _F32), jnp.zeros((rows, 1), _F32),
                         jnp.zeros((rows, HEAD_DIM), _F32))
        m_ref[...] = m
        l_ref[...] = l
        acc_ref[...] = acc

    m, l, acc = step([_page_heads(r) for r in kc], [_page_heads(r) for r in vc], bc_ref[...],
                     m_ref[...], l_ref[...], acc_ref[...])
    m_ref[...] = m
    l_ref[...] = l
    acc_ref[...] = acc

    @pl.when(p == n_steps - 1)
    def _():
        o_ref[...] = (acc / l * _silu(z_ref[...])).astype(o_ref.dtype)


def _dsa_sample(qs, kn, vn, cache_k, cache_v, bias_cache, bias_new, zs, page_table, layer):
    db, rows, _ = qs.shape
    ds = rows // DSA_HEADS
    n_pages = page_table.shape[1]
    pp = _pages_per_step(n_pages, (8, 4, 2, 1))
    n_steps = n_pages // pp
    w = DSA_WIDTH
    seq = lambda r, c: pl.BlockSpec((None, r, c), lambda b, p, pt: (b, 0, 0))
    pages = [pl.BlockSpec((None, None, DSA_HEADS, PAGE_SIZE, HEAD_DIM),
                          lambda b, p, pt, k=k: (layer, pt[b, p * pp + k], 0, 0, 0)) for k in range(pp)]
    return pl.pallas_call(
        functools.partial(_dsa_sample_kernel, ds=ds, n_steps=n_steps, pp=pp),
        grid_spec=pltpu.PrefetchScalarGridSpec(
            num_scalar_prefetch=1,
            grid=(db, n_steps),
            in_specs=[seq(rows, HEAD_DIM), seq(ds, w), seq(ds, w)] + pages + pages
                     + [seq(ds, PAGE_SIZE),
                        pl.BlockSpec((None, ds, pp * PAGE_SIZE), lambda b, p, pt: (b, 0, p)),
                        seq(rows, HEAD_DIM)],
            out_specs=seq(rows, HEAD_DIM),
            scratch_shapes=[pltpu.VMEM((rows, 1), _F32), pltpu.VMEM((rows, 1), _F32),
                            pltpu.VMEM((rows, HEAD_DIM), _F32)]),
        out_shape=jax.ShapeDtypeStruct(qs.shape, _BF16),
        compiler_params=_params("parallel", "arbitrary"),
        name="dsa_sample",
    )(page_table, qs, kn, vn, *([cache_k] * pp), *([cache_v] * pp), bias_new, bias_cache, zs)


def _rope_tables(pos, reps):
    def base(half):
        inv_freq = jnp.power(jnp.float32(ROPE_THETA), -jnp.arange(half, dtype=_F32) / half)
        ang = pos.astype(_F32)[:, None] * inv_freq[None, :]
        return jnp.cos(ang), jnp.sin(ang)

    tile = lambda a: jnp.tile(a, (reps, 1))
    c, s = base(HEAD_DIM // 2)
    head = (tile(jnp.concatenate([c, c], axis=1)), tile(jnp.concatenate([-s, s], axis=1)))
    c, s = base(IDX_DIM // 2)
    z = jnp.zeros_like(s)
    c64, lo64, hi64 = (jnp.concatenate([c, c], axis=1), jnp.concatenate([-s, z], axis=1),
                       jnp.concatenate([z, s], axis=1))
    idx = tuple(tile(jnp.concatenate([a, a], axis=1)) for a in (c64, lo64, hi64))
    one, zero = jnp.ones_like(c64), jnp.zeros_like(c64)
    kiwi = (tile(jnp.concatenate([c64, one], axis=1)), tile(jnp.concatenate([lo64, zero], axis=1)),
            tile(jnp.concatenate([hi64, zero], axis=1)))
    return head, idx, kiwi


_ROPE128 = (LANES // 2,)
_ROPE64 = (LANES - IDX_DIM // 2, IDX_DIM // 2)

_OFF = dict(q_sb=0, k_sb=768, v_sb=1536, z_sb=2304, q_d=3072, k_d=3840, v_d=4608, z_d=5376, qi=6144)
_KIWI0 = 7168
_TAIL0 = _KIWI0 + IDX_DIM + IDX_HEADS


def _project_all(h, w_main, w_kiwi, w_tail, layer, tabs, d_model, depth, st, kv_dtype, seq_len):
    head_t, idx_t, kiwi_t = tabs
    pj = functools.partial(_proj, h)
    kv = lambda name: [("heads_major", st[name], depth, seq_len), ("flat", kv_dtype)]
    o = {}
    (o["q_sb"],) = pj(w_main, layer, _OFF["q_sb"], SB_WIDTH, [("flat", _BF16)], name="proj_q_sb")
    st["k_sb"], o["k_sb"] = pj(w_main, layer, _OFF["k_sb"], SB_WIDTH, kv("k_sb"), name="proj_k_sb")
    st["v_sb"], o["v_sb"] = pj(w_main, layer, _OFF["v_sb"], SB_WIDTH, kv("v_sb"), name="proj_v_sb")
    (o["z_sb"],) = pj(w_main, layer, _OFF["z_sb"], SB_WIDTH, [("flat", _F32)], name="proj_z_sb")
    (o["q_d"],) = pj(w_main, layer, _OFF["q_d"], DSA_WIDTH, [("flat", _BF16)], rope=(_ROPE128, head_t),
                     name="proj_q_d")
    st["k_d"], o["k_d"] = pj(w_main, layer, _OFF["k_d"], DSA_WIDTH, kv("k_d"), rope=(_ROPE128, head_t),
                             name="proj_k_d")
    vd_req = [("heads_major", st["v_d"], depth, seq_len),
              ("flat_ones", _BF16) if kv_dtype == _BF16 else ("flat", kv_dtype)]
    st["v_d"], o["v_d"] = pj(w_main, layer, _OFF["v_d"], DSA_WIDTH, vd_req, name="proj_v_d")
    (o["z_d"],) = pj(w_main, layer, _OFF["z_d"], DSA_WIDTH, [("flat", _F32)], name="proj_z_d")
    (o["qi"],) = pj(w_main, layer, _OFF["qi"], IDX_WIDTH, [("flat", _BF16)], rope=(_ROPE64, idx_t), name="proj_qi")
    o["kiwi"], o["ki_b"], st["ki"] = pj(
        w_kiwi, layer, 0, LANES,
        [("flat", _F32), ("lanes", IDX_DIM, _BF16), ("lanes_stacked", IDX_DIM, st["ki"], depth)],
        rope=(_ROPE64, kiwi_t), name="proj_kiwi")
    (o["q_m"],) = pj(w_tail, layer, 0, MEM_WIDTH, [("flat", _F32)], name="proj_q_m")
    (o["z_m"],) = pj(w_tail, layer, MEM_WIDTH, MEM_WIDTH, [("flat", _F32)], name="proj_z_m")
    (o["gates"],) = pj(w_tail, layer, 2 * MEM_WIDTH, 3 * d_model, [("flat", _F32)], name="proj_gates")
    return o


def kernel(x_prompt, x_sample, cache_sb_k, cache_sb_v, cache_dsa_k, cache_dsa_v, cache_idx_k, cache_mem_k,
           cache_mem_v, page_table, mem_prompt, norm_g, w_in, w_up_sb, w_up_dsa, w_up_mem, w_out, w_mem_k,
           w_mem_v, final_norm_g):
    b, t, d = x_prompt.shape
    db, ds, _ = x_sample.shape
    depth = w_in.shape[0]
    n_pool = cache_sb_k.shape[1]
    n_pages = page_table.shape[1]
    n_past = n_pages * PAGE_SIZE
    mem_len = mem_prompt.shape[1]
    topk_p = min(TOPK_MAX, t // 4)
    topk_s = min(TOPK_MAX, (n_past + ds) // 4)
    assert t % Q_BLOCK == 0 and w_in.shape[2] == _TAIL0 + 2 * MEM_WIDTH + 3 * d

    w_main = w_in[:, :, :_KIWI0].astype(_BF16)
    w_kiwi = jnp.pad(w_in[:, :, _KIWI0:_TAIL0], ((0, 0), (0, 0), (0, LANES - (_TAIL0 - _KIWI0)))).astype(_BF16)
    w_tail = w_in[:, :, _TAIL0:].astype(_BF16)
    w_up_sb_b, w_up_dsa_b, w_up_mem_b = (w.astype(_BF16) for w in (w_up_sb, w_up_dsa, w_up_mem))
    w_out_b = w_out.astype(_BF16)
    w_mem_k_b, w_mem_v_b = w_mem_k.astype(_BF16), w_mem_v.astype(_BF16)
    gains = jnp.concatenate([norm_g, final_norm_g[None]], axis=0)

    tabs_p = _rope_tables(jnp.arange(t), b)
    tabs_s = _rope_tables(n_past + jnp.arange(ds), db)

    n = b * t
    x = x_prompt.reshape(n, d)
    h = _rmsnorm(x, gains[0], _BF16)
    mem_b = mem_prompt.reshape(b * mem_len, d).astype(_BF16)
    st_p = dict.fromkeys(("k_sb", "v_sb", "k_d", "v_d", "ki", "mk", "mv"))
    for l in range(depth):
        o = _project_all(h, w_main, w_kiwi, w_tail, l, tabs_p, d, depth, st_p, _BF16, t)
        g_sb = _sb_prompt(o["q_sb"], o["k_sb"], o["v_sb"], o["z_sb"], b, t)
        bias = _idx_prompt(o["qi"], o["kiwi"], o["ki_b"], b, t, topk_p)
        g_d = _dsa_prompt(o["q_d"], o["k_d"], o["v_d"], bias, o["z_d"], b, t)
        st_p["mk"], mk = _proj(mem_b, w_mem_k_b, l, 0, MEM_WIDTH, [("heads", st_p["mk"], depth), ("flat", _BF16)],
                               name="proj_mem_k")
        st_p["mv"], mv = _proj(mem_b, w_mem_v_b, l, 0, MEM_WIDTH, [("heads", st_p["mv"], depth), ("flat", _BF16)],
                               name="proj_mem_v")
        g_m = _mem_attend(o["q_m"].reshape(b, t, MEM_WIDTH), mk.reshape(b, mem_len, MEM_WIDTH),
                          mv.reshape(b, mem_len, MEM_WIDTH), o["z_m"].reshape(b, t, MEM_WIDTH))
        mixed = _mix(g_sb, g_d, g_m.reshape(n, MEM_WIDTH), o["gates"], w_up_sb_b, w_up_dsa_b, w_up_mem_b, l)
        last = l == depth - 1
        x, h = _out_proj(x, mixed, w_out_b, l, gains[l + 1], _F32 if last else _BF16)
    y_prompt = h.reshape(b, t, d)

    ns = db * ds
    x = x_sample.reshape(ns, d)
    h = _rmsnorm(x, gains[0], _BF16)

    def stack(a, heads):
        c = a.shape[1] // heads
        return a.reshape(db, ds, heads, c).transpose(0, 2, 1, 3).reshape(db, heads * ds, c)

    def unstack(a, heads):
        c = a.shape[2]
        return a.reshape(db, heads, ds, c).transpose(0, 2, 1, 3).reshape(db * ds, heads * c)

    by_head = lambda c: jnp.transpose(c, (0, 1, 3, 2, 4))
    c_sb_k, c_sb_v, c_d_k, c_d_v = (by_head(c) for c in (cache_sb_k, cache_sb_v, cache_dsa_k, cache_dsa_v))
    c_idx = jnp.transpose(cache_idx_k, (0, 1, 3, 2))

    st_s = dict.fromkeys(("k_sb", "v_sb", "k_d", "v_d", "ki"))
    for l in range(depth):
        o = _project_all(h, w_main, w_kiwi, w_tail, l, tabs_s, d, depth, st_s, _F32, ds)
        ki = o["kiwi"][:, :IDX_DIM]
        wi = o["kiwi"][:, IDX_DIM:IDX_DIM + IDX_HEADS]
        g_sb = _sb_sample(stack(o["q_sb"], SB_HEADS), o["k_sb"].reshape(db, ds, SB_WIDTH),
                          o["v_sb"].reshape(db, ds, SB_WIDTH), c_sb_k, c_sb_v,
                          stack(o["z_sb"], SB_HEADS), page_table, l)
        keys_c, keys_n = _idx_sample(stack(o["qi"], IDX_HEADS), stack(wi, IDX_HEADS),
                                     ki.reshape(db, ds, IDX_DIM), c_idx, page_table, l)
        bias_c, bias_n = _select_sample(keys_c.reshape(ns, n_past), keys_n.reshape(ns, PAGE_SIZE), topk_s)
        g_d = _dsa_sample(stack(o["q_d"], DSA_HEADS), o["k_d"].reshape(db, ds, DSA_WIDTH),
                          o["v_d"].reshape(db, ds, DSA_WIDTH), c_d_k, c_d_v,
                          bias_c.reshape(db, ds, n_past), bias_n.reshape(db, ds, PAGE_SIZE),
                          stack(o["z_d"], DSA_HEADS), page_table, l)
        g_m = _mem_attend(o["q_m"].reshape(db, ds, MEM_WIDTH), cache_mem_k, cache_mem_v,
                          o["z_m"].reshape(db, ds, MEM_WIDTH), layer=l)
        mixed = _mix(unstack(g_sb, SB_HEADS), unstack(g_d, DSA_HEADS), g_m.reshape(ns, MEM_WIDTH), o["gates"],
                     w_up_sb_b, w_up_dsa_b, w_up_mem_b, l)
        last = l == depth - 1
        x, h = _out_proj(x, mixed, w_out_b, l, gains[l + 1], _F32 if last else _BF16)
    y_sample = h.reshape(db, ds, d)

    kv = lambda a: jnp.transpose(a, (0, 1, 3, 2, 4))
    hp = lambda a: a.reshape((depth, b, -1) + a.shape[2:])
    hs = lambda a: a.reshape((depth, db, ds) + a.shape[2:])
    return (y_prompt, y_sample,
            kv(st_p["k_sb"]), kv(st_p["v_sb"]), kv(st_p["k_d"]), kv(st_p["v_d"]), hp(st_p["ki"]),
            hp(st_p["mk"]), hp(st_p["mv"]),
            kv(st_s["k_sb"]), kv(st_s["v_sb"]), kv(st_s["k_d"]), kv(st_s["v_d"]), hs(st_s["ki"]))
```
